```python
import jax, jax.numpy as jnp
from jax import lax
import numpy as np

D_MODEL = 1024
BATCH = 16
SEQ = 2048
DEPTH = 1

N_Q_HEADS = 8
N_KV_HEADS = 2
HEAD_DIM = 64
Q_GROUP = N_Q_HEADS // N_KV_HEADS
ATTN_WIDTH = N_Q_HEADS * HEAD_DIM
KV_WIDTH = N_KV_HEADS * HEAD_DIM
WINDOW = 128
BLOCK = 128
POOL_WINDOWS = (2, 4, 8, 16)
POOL_GROUPS = len(POOL_WINDOWS)
POOL_WIDTH = D_MODEL // 2
POOL_GROUP_DIM = POOL_WIDTH // POOL_GROUPS
IN_WIDTH = ATTN_WIDTH + 2 * KV_WIDTH + POOL_WIDTH
N_BRANCHES = 2
PEER_HEADS = 8
N_KEYS = 128
N_EXPERTS = N_KEYS * N_KEYS
PEER_HALF = 128
PEER_QUERY_DIM = 2 * PEER_HALF
PEER_TOPK = 16
PEER_CHUNK = 128
EPS = 1e-6
NEG_INF = -1e30

kernel_name = "hybrid_swa_pool_peer_encoder"


def _alibi_slopes():
    s = np.array([2.0 ** (-8.0 * (h + 1) / N_Q_HEADS) for h in range(N_Q_HEADS)], dtype=np.float32)
    return jnp.asarray(s).reshape(N_KV_HEADS, Q_GROUP)


def _rmsnorm(x, g):
    xf = x.astype(jnp.float32)
    y = xf * lax.rsqrt(jnp.mean(xf * xf, axis=-1, keepdims=True) + EPS)
    return (y * g.astype(jnp.float32)).astype(x.dtype)


def _windowed_gqa(q, k, v, sink_logits):
    B, S, _ = q.shape
    nb = S // BLOCK
    qb = q.reshape(B, nb, BLOCK, N_KV_HEADS, Q_GROUP, HEAD_DIM)

    def band(t):
        t = t.reshape(B, S, N_KV_HEADS, HEAD_DIM)
        tp = jnp.pad(t, ((0, 0), (BLOCK, BLOCK), (0, 0), (0, 0)))
        tp = tp.reshape(B, nb + 2, BLOCK, N_KV_HEADS, HEAD_DIM)
        return jnp.concatenate([tp[:, :-2], tp[:, 1:-1], tp[:, 2:]], axis=2)

    kb, vb = band(k), band(v)
    logits = jnp.einsum('bnqkgd,bnckd->bnkgqc', qb, kb).astype(jnp.float32) * (HEAD_DIM ** -0.5)
    qpos = jnp.arange(BLOCK)
    cpos = jnp.arange(3 * BLOCK)
    rel = qpos[:, None] + BLOCK - cpos[None, :]
    dist = jnp.abs(rel)
    s_abs = jnp.arange(nb)[:, None] * BLOCK - BLOCK + cpos[None, :]
    valid = (dist <= WINDOW)[None] & ((s_abs >= 0) & (s_abs < S))[:, None, :]
    logits = logits - _alibi_slopes()[:, :, None, None] * dist.astype(jnp.float32)
    logits = jnp.where(valid[None, :, None, None], logits, NEG_INF)
    sink = jnp.broadcast_to(sink_logits.astype(jnp.float32).reshape(N_KV_HEADS, Q_GROUP, 1, 1),
                            logits.shape[:-1] + (1,))
    probs = jax.nn.softmax(jnp.concatenate([logits, sink], axis=-1), axis=-1)[..., :-1]
    out = jnp.einsum('bnkgqc,bnckd->bnqkgd', probs.astype(v.dtype), vb)
    return out.reshape(B, S, ATTN_WIDTH)


def _pool_mixer(u, pool_w, pool_scale):
    B, S, _ = u.shape
    ug = u.reshape(B, S, POOL_GROUPS, POOL_GROUP_DIM)
    csum = jnp.concatenate([jnp.zeros((B, 1, POOL_GROUPS, POOL_GROUP_DIM), jnp.float32),
                            jnp.cumsum(ug.astype(jnp.float32), axis=1)], axis=1)
    t = jnp.arange(S)
    means = []
    for gi, w in enumerate(POOL_WINDOWS):
        lo = jnp.clip(t - w // 2, 0, S)
        hi = jnp.clip(t + w - w // 2, 0, S)
        total = csum[:, hi, gi] - csum[:, lo, gi]
        means.append(total / (hi - lo).astype(jnp.float32)[:, None])
    pooled = jnp.stack(means, axis=2)
    mixed = (pooled - ug.astype(jnp.float32)).astype(u.dtype)
    y = jnp.einsum('bsgc,gcd->bsgd', mixed, pool_w).reshape(B, S, POOL_WIDTH)
    return y * pool_scale


def _peer(h, w_query, sub_keys, u_table, v_table):
    B, S, D = h.shape
    q = (h @ w_query).reshape(B, S, PEER_HEADS, 2, PEER_HALF)
    scores = jnp.einsum('bshpc,hpnc->bshpn', q, sub_keys).astype(jnp.float32)
    top_s, top_i = lax.top_k(scores, PEER_TOPK)
    cand_s = (top_s[..., 0, :, None] + top_s[..., 1, None, :]).reshape(B, S, PEER_HEADS, PEER_TOPK * PEER_TOPK)
    cand_i = (top_i[..., 0, :, None] * N_KEYS + top_i[..., 1, None, :]).reshape(B, S, PEER_HEADS, PEER_TOPK * PEER_TOPK)
    best_s, best_pos = lax.top_k(cand_s, PEER_TOPK)
    expert_idx = jnp.take_along_axis(cand_i, best_pos, axis=-1)
    gate = jax.nn.softmax(best_s, axis=-1).astype(h.dtype)
    nc = (B * S) // PEER_CHUNK
    xs = h.reshape(nc, PEER_CHUNK, D)
    idx = expert_idx.reshape(nc, PEER_CHUNK, PEER_HEADS, PEER_TOPK)
    gs = gate.reshape(nc, PEER_CHUNK, PEER_HEADS, PEER_TOPK)

    def chunk(args):
        xc, ic, gc = args
        u_e = jnp.take(u_table, ic, axis=0)
        act = jax.nn.gelu(jnp.einsum('thkd,td->thk', u_e, xc), approximate=False) * gc
        v_e = jnp.take(v_table, ic, axis=0)
        return jnp.einsum('thk,thkd->td', act, v_e)

    return lax.map(chunk, (xs, idx, gs)).reshape(B, S, D)


def setup_inputs(seed: int = 0) -> dict:
    key = jax.random.key(seed)
    ks = jax.random.split(key, 17)
    nrm = lambda k, shape, s: jax.random.normal(k, shape, jnp.float32) * s
    L = DEPTH
    return {
        "x": nrm(ks[0], (BATCH, SEQ, D_MODEL), 1.0),
        "norm1_g": 1.0 + nrm(ks[1], (L, D_MODEL), 0.01),
        "w_in": nrm(ks[2], (L, D_MODEL, IN_WIDTH), D_MODEL ** -0.5),
        "sink_logits": nrm(ks[3], (L, N_Q_HEADS), 0.5),
        "pool_w": nrm(ks[4], (L, POOL_GROUPS, POOL_GROUP_DIM, POOL_GROUP_DIM), POOL_GROUP_DIM ** -0.5),
        "pool_scale": 1.0 + nrm(ks[5], (L, POOL_WIDTH), 0.1),
        "w_branch": nrm(ks[6], (L, N_BRANCHES, ATTN_WIDTH, D_MODEL), ATTN_WIDTH ** -0.5),
        "w_gate": nrm(ks[7], (L, D_MODEL, N_BRANCHES * D_MODEL), D_MODEL ** -0.5),
        "b_gate": nrm(ks[8], (L, N_BRANCHES * D_MODEL), 0.01),
        "w_out": nrm(ks[9], (L, D_MODEL, D_MODEL), D_MODEL ** -0.5),
        "norm2_g": 1.0 + nrm(ks[10], (L, D_MODEL), 0.01),
        "peer_w_query": nrm(ks[11], (L, D_MODEL, PEER_HEADS * PEER_QUERY_DIM), D_MODEL ** -0.5),
        "peer_sub_keys": nrm(ks[12], (L, PEER_HEADS, 2, N_KEYS, PEER_HALF), PEER_HALF ** -0.5),
        "peer_u": nrm(ks[13], (L, N_EXPERTS, D_MODEL), D_MODEL ** -0.5),
        "peer_v": nrm(ks[14], (L, N_EXPERTS, D_MODEL), 0.5),
        "norm_final_g": 1.0 + nrm(ks[15], (D_MODEL,), 0.01),
    }


def reference(x, norm1_g, w_in, sink_logits, pool_w, pool_scale, w_branch, w_gate, b_gate, w_out,
              norm2_g, peer_w_query, peer_sub_keys, peer_u, peer_v, norm_final_g):
    B, S, D = x.shape
    for l in range(DEPTH):
        h = _rmsnorm(x, norm1_g[l])
        proj = h @ w_in[l]
        q, k, v, u = jnp.split(proj, [ATTN_WIDTH, ATTN_WIDTH + KV_WIDTH, ATTN_WIDTH + 2 * KV_WIDTH], axis=-1)
        o_attn = _windowed_gqa(q, k, v, sink_logits[l])
        o_pool = _pool_mixer(u, pool_w[l], pool_scale[l])
        gates = jax.nn.sigmoid((h @ w_gate[l] + b_gate[l]).astype(jnp.float32)).astype(x.dtype)
        gates = gates.reshape(B, S, N_BRANCHES, D)
        merged = gates[:, :, 0] * (o_attn @ w_branch[l, 0]) + gates[:, :, 1] * (o_pool @ w_branch[l, 1])
        x = x + merged @ w_out[l]
        h2 = _rmsnorm(x, norm2_g[l])
        x = x + _peer(h2, peer_w_query[l], peer_sub_keys[l], peer_u[l], peer_v[l])
    return _rmsnorm(x, norm_final_g)
```

```python
import functools

import jax
import jax.numpy as jnp
from jax import lax
from jax.experimental import pallas as pl
from jax.experimental.pallas import tpu as pltpu

D_MODEL = 1024
N_Q_HEADS = 8
N_KV_HEADS = 2
HEAD_DIM = 64
Q_GROUP = N_Q_HEADS // N_KV_HEADS
ATTN_WIDTH = N_Q_HEADS * HEAD_DIM
KV_WIDTH = N_KV_HEADS * HEAD_DIM
WINDOW = 128
BLOCK = 128
POOL_WINDOWS = (2, 4, 8, 16)
POOL_WIDTH = D_MODEL // 2
POOL_GROUP_DIM = POOL_WIDTH // len(POOL_WINDOWS)
IN_WIDTH = ATTN_WIDTH + 2 * KV_WIDTH + POOL_WIDTH
PEER_HEADS = 8
N_KEYS = 128
N_EXPERTS = N_KEYS * N_KEYS
PEER_HALF = 128
PEER_TOPK = 16
N_PAIRS = PEER_HEADS * PEER_TOPK
EPS = 1e-6
NEG_INF = -1e30

SUBLANES = 8
LANES = 128
POOL_HALO = 8
ROW_VREGS = D_MODEL // LANES
TILE_STRIDE = N_PAIRS + 1

TM_PROJ = 512
TM_MIX = 256
TM_ROUTE = 256
TB_PEER = 64
VMEM_LIMIT_DENSE = 48 * 1024 * 1024
VMEM_LIMIT_PEER = 56 * 1024 * 1024

_F32 = jnp.float32
_BF16 = jnp.bfloat16


def _rmsnorm(x, g):
    return x * lax.rsqrt(jnp.mean(x * x, axis=-1, keepdims=True) + EPS) * g


def _split_bf16(a):
    hi = a.astype(_BF16)
    lo = (a - hi.astype(_F32)).astype(_BF16)
    return jnp.concatenate([hi, lo], axis=0)


def _in_proj_kernel(x_ref, g_ref, w_ref, q_ref, k_ref, v_ref, u_ref):
    h = _rmsnorm(x_ref[...], g_ref[...])
    proj = jnp.dot(h.astype(_BF16), w_ref[...], preferred_element_type=_F32)
    q_ref[...] = proj[:, :ATTN_WIDTH].astype(_BF16)
    k_ref[...] = proj[:, ATTN_WIDTH:ATTN_WIDTH + KV_WIDTH].astype(_BF16)
    v_ref[...] = proj[:, ATTN_WIDTH + KV_WIDTH:ATTN_WIDTH + 2 * KV_WIDTH].astype(_BF16)
    u_ref[...] = proj[:, ATTN_WIDTH + 2 * KV_WIDTH:]


def _in_proj(x2, g1, w_in):
    n = x2.shape[0]
    row = lambda w: pl.BlockSpec((TM_PROJ, w), lambda i: (i, 0))
    full = lambda a: pl.BlockSpec(a.shape, lambda i: (0,) * a.ndim)
    return pl.pallas_call(
        _in_proj_kernel,
        grid=(n // TM_PROJ,),
        in_specs=[row(D_MODEL), full(g1), full(w_in)],
        out_specs=[row(ATTN_WIDTH), row(KV_WIDTH), row(KV_WIDTH), row(POOL_WIDTH)],
        out_shape=[jax.ShapeDtypeStruct((n, ATTN_WIDTH), _BF16),
                   jax.ShapeDtypeStruct((n, KV_WIDTH), _BF16),
                   jax.ShapeDtypeStruct((n, KV_WIDTH), _BF16),
                   jax.ShapeDtypeStruct((n, POOL_WIDTH), _F32)],
        compiler_params=pltpu.CompilerParams(
            dimension_semantics=("arbitrary",), vmem_limit_bytes=VMEM_LIMIT_DENSE),
        name="in_proj",
    )(x2, g1, w_in)


def _attention_kernel(sink_ref, q_ref, kp_ref, kc_ref, kn_ref, vp_ref, vc_ref, vn_ref, o_ref, *, n_blocks):
    j = pl.program_id(1)
    rows = Q_GROUP * BLOCK
    qpos = lax.broadcasted_iota(jnp.int32, (rows, 3 * BLOCK), 0) % BLOCK
    cpos = lax.broadcasted_iota(jnp.int32, (rows, 3 * BLOCK), 1)
    grp = lax.broadcasted_iota(jnp.int32, (rows, 3 * BLOCK), 0) // BLOCK
    dist = jnp.abs(qpos + BLOCK - cpos)
    lo_c = jnp.where(j == 0, BLOCK, 0)
    hi_c = jnp.where(j == n_blocks - 1, 2 * BLOCK, 3 * BLOCK)
    valid = (dist <= WINDOW) & (cpos >= lo_c) & (cpos < hi_c)
    distf = dist.astype(_F32)
    row_grp = lax.broadcasted_iota(jnp.int32, (rows, 1), 0) // BLOCK
    q = q_ref[0]
    outs = []
    for kh in range(N_KV_HEADS):
        sl = slice(kh * HEAD_DIM, (kh + 1) * HEAD_DIM)
        kb = jnp.concatenate([kp_ref[0][:, sl], kc_ref[0][:, sl], kn_ref[0][:, sl]], axis=0)
        vb = jnp.concatenate([vp_ref[0][:, sl], vc_ref[0][:, sl], vn_ref[0][:, sl]], axis=0)
        qs = jnp.concatenate(
            [q[:, (kh * Q_GROUP + g) * HEAD_DIM:(kh * Q_GROUP + g + 1) * HEAD_DIM] for g in range(Q_GROUP)],
            axis=0)
        logits = lax.dot_general(qs, kb, (((1,), (1,)), ((), ())),
                                 preferred_element_type=_F32) * (HEAD_DIM ** -0.5)
        slope = jnp.zeros((rows, 3 * BLOCK), _F32)
        sink = jnp.zeros((rows, 1), _F32)
        for g in range(Q_GROUP):
            head = kh * Q_GROUP + g
            slope = jnp.where(grp == g, 2.0 ** (-8.0 * (head + 1) / N_Q_HEADS), slope)
            sink = jnp.where(row_grp == g, sink_ref[head], sink)
        logits = jnp.where(valid, logits - slope * distf, NEG_INF)
        m = jnp.maximum(jnp.max(logits, axis=-1, keepdims=True), sink)
        p = jnp.exp(logits - m)
        denom = jnp.sum(p, axis=-1, keepdims=True) + jnp.exp(sink - m)
        o = jnp.dot(p.astype(_BF16), vb, preferred_element_type=_F32) / denom
        outs += [o[g * BLOCK:(g + 1) * BLOCK] for g in range(Q_GROUP)]
    o_ref[0] = jnp.concatenate(outs, axis=-1).astype(_BF16)


def _attention(q3, k3, v3, sink):
    b, s, _ = q3.shape
    nb = s // BLOCK
    cur = lambda w: pl.BlockSpec((1, BLOCK, w), lambda bi, j: (bi, j, 0))
    prv = lambda w: pl.BlockSpec((1, BLOCK, w), lambda bi, j: (bi, jnp.maximum(j - 1, 0), 0))
    nxt = lambda w: pl.BlockSpec((1, BLOCK, w), lambda bi, j: (bi, jnp.minimum(j + 1, nb - 1), 0))
    return pl.pallas_call(
        functools.partial(_attention_kernel, n_blocks=nb),
        grid=(b, nb),
        in_specs=[pl.BlockSpec(memory_space=pltpu.SMEM), cur(ATTN_WIDTH),
                  prv(KV_WIDTH), cur(KV_WIDTH), nxt(KV_WIDTH),
                  prv(KV_WIDTH), cur(KV_WIDTH), nxt(KV_WIDTH)],
        out_specs=cur(ATTN_WIDTH),
        out_shape=jax.ShapeDtypeStruct((b, s, ATTN_WIDTH), _BF16),
        compiler_params=pltpu.CompilerParams(
            dimension_semantics=("arbitrary", "arbitrary"), vmem_limit_bytes=VMEM_LIMIT_DENSE),
        name="attention",
    )(sink, q3, k3, k3, k3, v3, v3, v3)


def _shift_rows(a, k):
    n = a.shape[0]
    return pltpu.roll(a, (-k) % n, axis=0)


def _mix_kernel(x_ref, u_ref, up_ref, un_ref, oa_ref, g1_ref, wg_ref, bg_ref, pw_ref, ps_ref,
                wb_ref, wo_ref, g2_ref, x1_ref, h2_ref, *, seq):
    i = pl.program_id(0)
    t0 = (i * TM_MIX) % seq
    x = x_ref[...]
    h = _rmsnorm(x, g1_ref[...]).astype(_BF16)

    prev = jnp.where(t0 > 0, up_ref[...], 0.0)
    nxt = jnp.where(t0 + TM_MIX < seq, un_ref[...], 0.0)
    ue = jnp.concatenate([prev, u_ref[...], nxt], axis=0)
    t = t0 + lax.broadcasted_iota(jnp.int32, (TM_MIX, 1), 0)
    centre = slice(POOL_HALO, POOL_HALO + TM_MIX)
    pooled = []
    for gi, w in enumerate(POOL_WINDOWS):
        ug = ue[:, gi * POOL_GROUP_DIM:(gi + 1) * POOL_GROUP_DIM]
        acc = ug + _shift_rows(ug, -1)
        half = 1
        while 2 * half < w:
            acc = _shift_rows(acc, half) + _shift_rows(acc, -half)
            half *= 2
        cnt = (jnp.minimum(t + (w - w // 2), seq) - jnp.maximum(t - w // 2, 0)).astype(_F32)
        mixed = acc[centre] / cnt - ug[centre]
        y = jnp.dot(mixed.astype(_BF16), pw_ref[gi], preferred_element_type=_F32)
        pooled.append(y)
    o_pool = jnp.concatenate(pooled, axis=-1) * ps_ref[...]

    gates = jax.nn.sigmoid(jnp.dot(h, wg_ref[...], preferred_element_type=_F32) + bg_ref[...])
    br_a = jnp.dot(oa_ref[...], wb_ref[0], preferred_element_type=_F32)
    br_b = jnp.dot(o_pool.astype(_BF16), wb_ref[1], preferred_element_type=_F32)
    merged = gates[:, :D_MODEL] * br_a + gates[:, D_MODEL:] * br_b
    x1 = x + jnp.dot(merged.astype(_BF16), wo_ref[...], preferred_element_type=_F32)
    x1_ref[...] = x1
    h2_ref[...] = _rmsnorm(x1, g2_ref[...])


def _mix(x2, u, o_attn, g1, w_gate, b_gate, pool_w, pool_scale, w_branch, w_out, g2, seq):
    n = x2.shape[0]
    per_halo = TM_MIX // POOL_HALO
    n_halo = n // POOL_HALO
    row = lambda w: pl.BlockSpec((TM_MIX, w), lambda i: (i, 0))
    full = lambda a: pl.BlockSpec(a.shape, lambda i: (0,) * a.ndim)
    halo_prev = pl.BlockSpec((POOL_HALO, POOL_WIDTH), lambda i: (jnp.maximum(i * per_halo - 1, 0), 0))
    halo_next = pl.BlockSpec((POOL_HALO, POOL_WIDTH), lambda i: (jnp.minimum((i + 1) * per_halo, n_halo - 1), 0))
    return pl.pallas_call(
        functools.partial(_mix_kernel, seq=seq),
        grid=(n // TM_MIX,),
        in_specs=[row(D_MODEL), row(POOL_WIDTH), halo_prev, halo_next, row(ATTN_WIDTH),
                  full(g1), full(w_gate), full(b_gate), full(pool_w), full(pool_scale),
                  full(w_branch), full(w_out), full(g2)],
        out_specs=[row(D_MODEL), row(D_MODEL)],
        out_shape=[jax.ShapeDtypeStruct((n, D_MODEL), _F32), jax.ShapeDtypeStruct((n, D_MODEL), _F32)],
        compiler_params=pltpu.CompilerParams(
            dimension_semantics=("arbitrary",), vmem_limit_bytes=VMEM_LIMIT_DENSE),
        name="mix",
    )(x2, u, u, u, o_attn, g1, w_gate, b_gate, pool_w, pool_scale, w_branch, w_out, g2)


def _top_rows(s, payload, k):
    n = s.shape[0]
    iota = lax.broadcasted_iota(jnp.int32, s.shape, 0)
    vals, picks = [], []
    for _ in range(k):
        m = jnp.max(s, axis=0, keepdims=True)
        am = jnp.min(jnp.where(s == m, iota, n), axis=0, keepdims=True)
        hit = iota == am
        vals.append(m)
        picks.append(am if payload is None else jnp.max(jnp.where(hit, payload, -1), axis=0, keepdims=True))
        s = jnp.where(hit, -jnp.inf, s)
    return jnp.concatenate(vals, axis=0), jnp.concatenate(picks, axis=0)


def _route_kernel(h2_ref, wq_ref, keys_ref, idx_ref, gate_ref):
    qp = jnp.dot(h2_ref[...].astype(_BF16), wq_ref[...], preferred_element_type=_F32)
    e_rows, g_rows = [], []
    for hd in range(PEER_HEADS):
        tops, topi = [], []
        for p in range(2):
            c0 = (hd * 2 + p) * PEER_HALF
            qhp = qp[:, c0:c0 + PEER_HALF].astype(_BF16)
            sc = lax.dot_general(keys_ref[hd, p], qhp, (((1,), (1,)), ((), ())),
                                 preferred_element_type=_F32)
            ts, ti = _top_rows(sc, None, PEER_TOPK)
            tops.append(ts)
            topi.append(ti)
        cand_s = jnp.concatenate([tops[0][a:a + 1] + tops[1] for a in range(PEER_TOPK)], axis=0)
        cand_i = jnp.concatenate([topi[0][a:a + 1] * N_KEYS + topi[1] for a in range(PEER_TOPK)], axis=0)
        best_s, best_e = _top_rows(cand_s, cand_i, PEER_TOPK)
        ex = jnp.exp(best_s - jnp.max(best_s, axis=0, keepdims=True))
        g_rows.append(ex / jnp.sum(ex, axis=0, keepdims=True))
        e_rows.append(best_e)
    idx_ref[...] = jnp.concatenate(e_rows, axis=0).T
    gate_ref[...] = jnp.concatenate(g_rows, axis=0).T


def _route(h2, w_query, sub_keys):
    n = h2.shape[0]
    full = lambda a: pl.BlockSpec(a.shape, lambda i: (0,) * a.ndim)
    return pl.pallas_call(
        _route_kernel,
        grid=(n // TM_ROUTE,),
        in_specs=[pl.BlockSpec((TM_ROUTE, D_MODEL), lambda i: (i, 0)), full(w_query), full(sub_keys)],
        out_specs=[pl.BlockSpec((TM_ROUTE, N_PAIRS), lambda i: (i, 0)),
                   pl.BlockSpec((TM_ROUTE, N_PAIRS), lambda i: (i, 0))],
        out_shape=[jax.ShapeDtypeStruct((n, N_PAIRS), jnp.int32), jax.ShapeDtypeStruct((n, N_PAIRS), _F32)],
        compiler_params=pltpu.CompilerParams(
            dimension_semantics=("arbitrary",), vmem_limit_bytes=VMEM_LIMIT_DENSE),
        name="route",
    )(h2, w_query, sub_keys)


def _gather_rows(idx_ref, t, tab_ref, tile_ref):
    for k in range(N_PAIRS):
        tile_ref[pl.ds(k, ROW_VREGS, stride=TILE_STRIDE), :] = tab_ref[idx_ref[t, k]].astype(_F32)
    return jnp.concatenate(
        [tile_ref[pl.ds(r * TILE_STRIDE, N_PAIRS), :] for r in range(ROW_VREGS)], axis=-1).astype(_BF16)


def _peer_up_kernel(idx_ref, h2_ref, gate_ref, tab_ref, act_ref, tile_ref, s_ref):
    def token(t, carry):
        rows = _gather_rows(idx_ref, t, tab_ref, tile_ref)
        lhs = _split_bf16(h2_ref[pl.ds(t, 1), :])
        s2 = lax.dot_general(lhs, rows, (((1,), (1,)), ((), ())), preferred_element_type=_F32)
        s_ref[pl.ds(t, 1), :] = s2[0:1] + s2[1:2]
        return carry

    lax.fori_loop(0, TB_PEER, token, 0)
    s = s_ref[...]
    act_ref[...] = 0.5 * s * (1.0 + lax.erf(s * (2.0 ** -0.5))) * gate_ref[...]


def _peer_down_kernel(idx_ref, act_ref, x1_ref, tab_ref, g_ref, o_ref, tile_ref, y_ref):
    def token(t, carry):
        rows = _gather_rows(idx_ref, t, tab_ref, tile_ref)
        lhs = _split_bf16(act_ref[pl.ds(t, 1), :])
        y2 = jnp.dot(lhs, rows, preferred_element_type=_F32)
        y_ref[pl.ds(t, 1), :] = y2[0:1] + y2[1:2]
        return carry

    lax.fori_loop(0, TB_PEER, token, 0)
    o_ref[...] = _rmsnorm(x1_ref[...] + y_ref[...], g_ref[...])


def _peer_specs():
    tok = lambda w: pl.BlockSpec((TB_PEER, w), lambda i: (i, 0))
    idx = pl.BlockSpec((TB_PEER, N_PAIRS), lambda i: (i, 0), memory_space=pltpu.SMEM)
    table = pl.BlockSpec(memory_space=pltpu.VMEM)
    tile = pltpu.VMEM((ROW_VREGS * TILE_STRIDE, LANES), _F32)
    params = pltpu.CompilerParams(dimension_semantics=("arbitrary",), vmem_limit_bytes=VMEM_LIMIT_PEER)
    return tok, idx, table, tile, params


def _peer_up(idx, h2, gate, table):
    n = h2.shape[0]
    tok, idx_spec, table_spec, tile, params = _peer_specs()
    return pl.pallas_call(
        _peer_up_kernel,
        grid=(n // TB_PEER,),
        in_specs=[idx_spec, tok(D_MODEL), tok(N_PAIRS), table_spec],
        out_specs=tok(N_PAIRS),
        out_shape=jax.ShapeDtypeStruct((n, N_PAIRS), _F32),
        scratch_shapes=[tile, pltpu.VMEM((TB_PEER, N_PAIRS), _F32)],
        compiler_params=params,
        name="peer_up",
    )(idx, h2, gate, table)


def _peer_down(idx, act, x1, table, g_final):
    n = x1.shape[0]
    tok, idx_spec, table_spec, tile, params = _peer_specs()
    return pl.pallas_call(
        _peer_down_kernel,
        grid=(n // TB_PEER,),
        in_specs=[idx_spec, tok(N_PAIRS), tok(D_MODEL), table_spec,
                  pl.BlockSpec(g_final.shape, lambda i: (0, 0))],
        out_specs=tok(D_MODEL),
        out_shape=jax.ShapeDtypeStruct((n, D_MODEL), _F32),
        scratch_shapes=[tile, pltpu.VMEM((TB_PEER, D_MODEL), _F32)],
        compiler_params=params,
        name="peer_down",
    )(idx, act, x1, table, g_final)


def _expert_table(t):
    return t.astype(_BF16).reshape(N_EXPERTS, ROW_VREGS, LANES)


def kernel(x, norm1_g, w_in, sink_logits, pool_w, pool_scale, w_branch, w_gate, b_gate, w_out,
           norm2_g, peer_w_query, peer_sub_keys, peer_u, peer_v, norm_final_g):
    b, s, d = x.shape
    n = b * s
    assert norm1_g.shape[0] == 1 and d == D_MODEL
    assert s % TM_MIX == 0 and s % BLOCK == 0 and n % TM_PROJ == 0 and n % TM_ROUTE == 0 and n % TB_PEER == 0
    row = lambda a: a.reshape(1, -1)
    x2 = x.reshape(n, d)
    g1 = row(norm1_g[0])
    q, k, v, u = _in_proj(x2, g1, w_in[0].astype(_BF16))
    o_attn = _attention(q.reshape(b, s, ATTN_WIDTH), k.reshape(b, s, KV_WIDTH),
                        v.reshape(b, s, KV_WIDTH), sink_logits[0]).reshape(n, ATTN_WIDTH)
    x1, h2 = _mix(x2, u, o_attn, g1, w_gate[0].astype(_BF16), row(b_gate[0]),
                  pool_w[0].astype(_BF16), row(pool_scale[0]), w_branch[0].astype(_BF16),
                  w_out[0].astype(_BF16), row(norm2_g[0]), s)
    idx, gate = _route(h2, peer_w_query[0].astype(_BF16), peer_sub_keys[0].astype(_BF16))
    act = _peer_up(idx, h2, gate, _expert_table(peer_u[0]))
    y = _peer_down(idx, act, x1, _expert_table(peer_v[0]), row(norm_final_g))
    return y.reshape(b, s, d)
```

```python
import functools

import jax
import jax.numpy as jnp
from jax import lax
from jax.experimental import pallas as pl
from jax.experimental.pallas import tpu as pltpu

D_MODEL = 1024
N_Q_HEADS = 8
N_KV_HEADS = 2
HEAD_DIM = 64
Q_GROUP = N_Q_HEADS // N_KV_HEADS
ATTN_WIDTH = N_Q_HEADS * HEAD_DIM
KV_WIDTH = N_KV_HEADS * HEAD_DIM
WINDOW = 128
BLOCK = 128
POOL_WINDOWS = (2, 4, 8, 16)
POOL_WIDTH = D_MODEL // 2
POOL_GROUP_DIM = POOL_WIDTH // len(POOL_WINDOWS)
IN_WIDTH = ATTN_WIDTH + 2 * KV_WIDTH + POOL_WIDTH
PEER_HEADS = 8
N_KEYS = 128
N_EXPERTS = N_KEYS * N_KEYS
PEER_HALF = 128
PEER_TOPK = 16
N_PAIRS = PEER_HEADS * PEER_TOPK
EPS = 1e-6
NEG_INF = -1e30

SUBLANES = 8
LANES = 128
POOL_HALO = 8
ROW_VREGS = D_MODEL // LANES
TABLE_ROWS_PER_EXPERT = ROW_VREGS // 2
TILE_STRIDE = N_PAIRS + 1

TM_PROJ = 512
TM_MIX = 256
TM_ROUTE = 256
TB_PEER = 128
VMEM_LIMIT_DENSE = 48 * 1024 * 1024
VMEM_LIMIT_PEER = 56 * 1024 * 1024

_F32 = jnp.float32
_BF16 = jnp.bfloat16


def _rmsnorm(x, g):
    return x * lax.rsqrt(jnp.mean(x * x, axis=-1, keepdims=True) + EPS) * g


def _split_bf16(a):
    hi = a.astype(_BF16)
    lo = (a - hi.astype(_F32)).astype(_BF16)
    return jnp.concatenate([hi, lo], axis=0)


def _in_proj_kernel(x_ref, g_ref, w_ref, q_ref, k_ref, v_ref, u_ref):
    h = _rmsnorm(x_ref[...], g_ref[...])
    proj = jnp.dot(h.astype(_BF16), w_ref[...], preferred_element_type=_F32)
    q_ref[...] = proj[:, :ATTN_WIDTH].astype(_BF16)
    k_ref[...] = proj[:, ATTN_WIDTH:ATTN_WIDTH + KV_WIDTH].astype(_BF16)
    v_ref[...] = proj[:, ATTN_WIDTH + KV_WIDTH:ATTN_WIDTH + 2 * KV_WIDTH].astype(_BF16)
    u_ref[...] = proj[:, ATTN_WIDTH + 2 * KV_WIDTH:]


def _in_proj(x2, g1, w_in):
    n = x2.shape[0]
    row = lambda w: pl.BlockSpec((TM_PROJ, w), lambda i: (i, 0))
    full = lambda a: pl.BlockSpec(a.shape, lambda i: (0,) * a.ndim)
    return pl.pallas_call(
        _in_proj_kernel,
        grid=(n // TM_PROJ,),
        in_specs=[row(D_MODEL), full(g1), full(w_in)],
        out_specs=[row(ATTN_WIDTH), row(KV_WIDTH), row(KV_WIDTH), row(POOL_WIDTH)],
        out_shape=[jax.ShapeDtypeStruct((n, ATTN_WIDTH), _BF16),
                   jax.ShapeDtypeStruct((n, KV_WIDTH), _BF16),
                   jax.ShapeDtypeStruct((n, KV_WIDTH), _BF16),
                   jax.ShapeDtypeStruct((n, POOL_WIDTH), _F32)],
        compiler_params=pltpu.CompilerParams(
            dimension_semantics=("arbitrary",), vmem_limit_bytes=VMEM_LIMIT_DENSE),
        name="in_proj",
    )(x2, g1, w_in)


def _attention_kernel(sink_ref, q_ref, kp_ref, kc_ref, kn_ref, vp_ref, vc_ref, vn_ref, o_ref, *, n_blocks):
    j = pl.program_id(1)
    rows = Q_GROUP * BLOCK
    qpos = lax.broadcasted_iota(jnp.int32, (rows, 3 * BLOCK), 0) % BLOCK
    cpos = lax.broadcasted_iota(jnp.int32, (rows, 3 * BLOCK), 1)
    grp = lax.broadcasted_iota(jnp.int32, (rows, 3 * BLOCK), 0) // BLOCK
    dist = jnp.abs(qpos + BLOCK - cpos)
    lo_c = jnp.where(j == 0, BLOCK, 0)
    hi_c = jnp.where(j == n_blocks - 1, 2 * BLOCK, 3 * BLOCK)
    valid = (dist <= WINDOW) & (cpos >= lo_c) & (cpos < hi_c)
    distf = dist.astype(_F32)
    row_grp = lax.broadcasted_iota(jnp.int32, (rows, 1), 0) // BLOCK
    q = q_ref[0]
    outs = []
    for kh in range(N_KV_HEADS):
        sl = slice(kh * HEAD_DIM, (kh + 1) * HEAD_DIM)
        kb = jnp.concatenate([kp_ref[0][:, sl], kc_ref[0][:, sl], kn_ref[0][:, sl]], axis=0)
        vb = jnp.concatenate([vp_ref[0][:, sl], vc_ref[0][:, sl], vn_ref[0][:, sl]], axis=0)
        qs = jnp.concatenate(
            [q[:, (kh * Q_GROUP + g) * HEAD_DIM:(kh * Q_GROUP + g + 1) * HEAD_DIM] for g in range(Q_GROUP)],
            axis=0)
        logits = lax.dot_general(qs, kb, (((1,), (1,)), ((), ())),
                                 preferred_element_type=_F32) * (HEAD_DIM ** -0.5)
        slope = jnp.zeros((rows, 3 * BLOCK), _F32)
        sink = jnp.zeros((rows, 1), _F32)
        for g in range(Q_GROUP):
            head = kh * Q_GROUP + g
            slope = jnp.where(grp == g, 2.0 ** (-8.0 * (head + 1) / N_Q_HEADS), slope)
            sink = jnp.where(row_grp == g, sink_ref[head], sink)
        logits = jnp.where(valid, logits - slope * distf, NEG_INF)
        m = jnp.maximum(jnp.max(logits, axis=-1, keepdims=True), sink)
        p = jnp.exp(logits - m)
        denom = jnp.sum(p, axis=-1, keepdims=True) + jnp.exp(sink - m)
        o = jnp.dot(p.astype(_BF16), vb, preferred_element_type=_F32) / denom
        outs += [o[g * BLOCK:(g + 1) * BLOCK] for g in range(Q_GROUP)]
    o_ref[0] = jnp.concatenate(outs, axis=-1).astype(_BF16)


def _attention(q3, k3, v3, sink):
    b, s, _ = q3.shape
    nb = s // BLOCK
    cur = lambda w: pl.BlockSpec((1, BLOCK, w), lambda bi, j: (bi, j, 0))
    prv = lambda w: pl.BlockSpec((1, BLOCK, w), lambda bi, j: (bi, jnp.maximum(j - 1, 0), 0))
    nxt = lambda w: pl.BlockSpec((1, BLOCK, w), lambda bi, j: (bi, jnp.minimum(j + 1, nb - 1), 0))
    return pl.pallas_call(
        functools.partial(_attention_kernel, n_blocks=nb),
        grid=(b, nb),
        in_specs=[pl.BlockSpec(memory_space=pltpu.SMEM), cur(ATTN_WIDTH),
                  prv(KV_WIDTH), cur(KV_WIDTH), nxt(KV_WIDTH),
                  prv(KV_WIDTH), cur(KV_WIDTH), nxt(KV_WIDTH)],
        out_specs=cur(ATTN_WIDTH),
        out_shape=jax.ShapeDtypeStruct((b, s, ATTN_WIDTH), _BF16),
        compiler_params=pltpu.CompilerParams(
            dimension_semantics=("arbitrary", "arbitrary"), vmem_limit_bytes=VMEM_LIMIT_DENSE),
        name="attention",
    )(sink, q3, k3, k3, k3, v3, v3, v3)


def _shift_rows(a, k):
    n = a.shape[0]
    return pltpu.roll(a, (-k) % n, axis=0)


def _mix_kernel(x_ref, u_ref, up_ref, un_ref, oa_ref, g1_ref, wg_ref, bg_ref, pw_ref, ps_ref,
                wb_ref, wo_ref, g2_ref, x1_ref, h2_ref, *, seq):
    i = pl.program_id(0)
    t0 = (i * TM_MIX) % seq
    x = x_ref[...]
    h = _rmsnorm(x, g1_ref[...]).astype(_BF16)

    prev = jnp.where(t0 > 0, up_ref[...], 0.0)
    nxt = jnp.where(t0 + TM_MIX < seq, un_ref[...], 0.0)
    ue = jnp.concatenate([prev, u_ref[...], nxt], axis=0)
    t = t0 + lax.broadcasted_iota(jnp.int32, (TM_MIX, 1), 0)
    centre = slice(POOL_HALO, POOL_HALO + TM_MIX)
    pooled = []
    for gi, w in enumerate(POOL_WINDOWS):
        ug = ue[:, gi * POOL_GROUP_DIM:(gi + 1) * POOL_GROUP_DIM]
        acc = ug + _shift_rows(ug, -1)
        half = 1
        while 2 * half < w:
            acc = _shift_rows(acc, half) + _shift_rows(acc, -half)
            half *= 2
        cnt = (jnp.minimum(t + (w - w // 2), seq) - jnp.maximum(t - w // 2, 0)).astype(_F32)
        mixed = acc[centre] / cnt - ug[centre]
        y = jnp.dot(mixed.astype(_BF16), pw_ref[gi], preferred_element_type=_F32)
        pooled.append(y)
    o_pool = jnp.concatenate(pooled, axis=-1) * ps_ref[...]

    gates = jax.nn.sigmoid(jnp.dot(h, wg_ref[...], preferred_element_type=_F32) + bg_ref[...])
    br_a = jnp.dot(oa_ref[...], wb_ref[0], preferred_element_type=_F32)
    br_b = jnp.dot(o_pool.astype(_BF16), wb_ref[1], preferred_element_type=_F32)
    merged = gates[:, :D_MODEL] * br_a + gates[:, D_MODEL:] * br_b
    x1 = x + jnp.dot(merged.astype(_BF16), wo_ref[...], preferred_element_type=_F32)
    x1_ref[...] = x1
    h2_ref[...] = _rmsnorm(x1, g2_ref[...])


def _mix(x2, u, o_attn, g1, w_gate, b_gate, pool_w, pool_scale, w_branch, w_out, g2, seq):
    n = x2.shape[0]
    per_halo = TM_MIX // POOL_HALO
    n_halo = n // POOL_HALO
    row = lambda w: pl.BlockSpec((TM_MIX, w), lambda i: (i, 0))
    full = lambda a: pl.BlockSpec(a.shape, lambda i: (0,) * a.ndim)
    halo_prev = pl.BlockSpec((POOL_HALO, POOL_WIDTH), lambda i: (jnp.maximum(i * per_halo - 1, 0), 0))
    halo_next = pl.BlockSpec((POOL_HALO, POOL_WIDTH), lambda i: (jnp.minimum((i + 1) * per_halo, n_halo - 1), 0))
    return pl.pallas_call(
        functools.partial(_mix_kernel, seq=seq),
        grid=(n // TM_MIX,),
        in_specs=[row(D_MODEL), row(POOL_WIDTH), halo_prev, halo_next, row(ATTN_WIDTH),
                  full(g1), full(w_gate), full(b_gate), full(pool_w), full(pool_scale),
                  full(w_branch), full(w_out), full(g2)],
        out_specs=[row(D_MODEL), row(D_MODEL)],
        out_shape=[jax.ShapeDtypeStruct((n, D_MODEL), _F32), jax.ShapeDtypeStruct((n, D_MODEL), _F32)],
        compiler_params=pltpu.CompilerParams(
            dimension_semantics=("arbitrary",), vmem_limit_bytes=VMEM_LIMIT_DENSE),
        name="mix",
    )(x2, u, u, u, o_attn, g1, w_gate, b_gate, pool_w, pool_scale, w_branch, w_out, g2)


def _top_rows(s, payload, k):
    n = s.shape[0]
    iota = lax.broadcasted_iota(jnp.int32, s.shape, 0)
    vals, picks = [], []
    for _ in range(k):
        m = jnp.max(s, axis=0, keepdims=True)
        am = jnp.min(jnp.where(s == m, iota, n), axis=0, keepdims=True)
        hit = iota == am
        vals.append(m)
        picks.append(am if payload is None else jnp.max(jnp.where(hit, payload, -1), axis=0, keepdims=True))
        s = jnp.where(hit, -jnp.inf, s)
    return jnp.concatenate(vals, axis=0), jnp.concatenate(picks, axis=0)


def _route_kernel(h2_ref, wq_ref, keys_ref, idx_ref, gate_ref):
    qp = jnp.dot(h2_ref[...].astype(_BF16), wq_ref[...], preferred_element_type=_F32)
    e_rows, g_rows = [], []
    for hd in range(PEER_HEADS):
        tops, topi = [], []
        for p in range(2):
            c0 = (hd * 2 + p) * PEER_HALF
            qhp = qp[:, c0:c0 + PEER_HALF].astype(_BF16)
            sc = lax.dot_general(keys_ref[hd, p], qhp, (((1,), (1,)), ((), ())),
                                 preferred_element_type=_F32)
            ts, ti = _top_rows(sc, None, PEER_TOPK)
            tops.append(ts)
            topi.append(ti)
        cand_s = jnp.concatenate([tops[0][a:a + 1] + tops[1] for a in range(PEER_TOPK)], axis=0)
        cand_i = jnp.concatenate(
            [(topi[0][a:a + 1] * N_KEYS + topi[1]) * TABLE_ROWS_PER_EXPERT for a in range(PEER_TOPK)], axis=0)
        best_s, best_e = _top_rows(cand_s, cand_i, PEER_TOPK)
        ex = jnp.exp(best_s - jnp.max(best_s, axis=0, keepdims=True))
        g_rows.append(ex / jnp.sum(ex, axis=0, keepdims=True))
        e_rows.append(best_e)
    idx_ref[...] = jnp.concatenate(e_rows, axis=0).T
    gate_ref[...] = jnp.concatenate(g_rows, axis=0).T


def _route(h2, w_query, sub_keys):
    n = h2.shape[0]
    full = lambda a: pl.BlockSpec(a.shape, lambda i: (0,) * a.ndim)
    return pl.pallas_call(
        _route_kernel,
        grid=(n // TM_ROUTE,),
        in_specs=[pl.BlockSpec((TM_ROUTE, D_MODEL), lambda i: (i, 0)), full(w_query), full(sub_keys)],
        out_specs=[pl.BlockSpec((TM_ROUTE, N_PAIRS), lambda i: (i, 0)),
                   pl.BlockSpec((TM_ROUTE, N_PAIRS), lambda i: (i, 0))],
        out_shape=[jax.ShapeDtypeStruct((n, N_PAIRS), jnp.int32), jax.ShapeDtypeStruct((n, N_PAIRS), _F32)],
        compiler_params=pltpu.CompilerParams(
            dimension_semantics=("arbitrary",), vmem_limit_bytes=VMEM_LIMIT_DENSE),
        name="route",
    )(h2, w_query, sub_keys)


def _gather_to_tile(idx_ref, t, tab_ref, tile_ref):
    row_ref = idx_ref.at[t]
    for k in range(N_PAIRS):
        r = pl.multiple_of(row_ref[k], TABLE_ROWS_PER_EXPERT)
        slab = pltpu.bitcast(tab_ref[pl.ds(r, TABLE_ROWS_PER_EXPERT), :], _BF16)
        tile_ref[pl.ds(k, ROW_VREGS, stride=TILE_STRIDE), :] = slab.astype(_F32)


def _tile_rows(tile_ref):
    return jnp.concatenate(
        [tile_ref[pl.ds(r * TILE_STRIDE, N_PAIRS), :] for r in range(ROW_VREGS)], axis=-1).astype(_BF16)


def _token_pipeline(idx_ref, tab_ref, tile_a, tile_b, consume):
    _gather_to_tile(idx_ref, 0, tab_ref, tile_a)
    _gather_to_tile(idx_ref, 1, tab_ref, tile_b)

    def pair(j, carry):
        t0 = 2 * j
        consume(t0, _tile_rows(tile_a))
        _gather_to_tile(idx_ref, t0 + 2, tab_ref, tile_a)
        consume(t0 + 1, _tile_rows(tile_b))
        _gather_to_tile(idx_ref, t0 + 3, tab_ref, tile_b)
        return carry

    lax.fori_loop(0, TB_PEER // 2 - 1, pair, 0)
    consume(TB_PEER - 2, _tile_rows(tile_a))
    consume(TB_PEER - 1, _tile_rows(tile_b))


def _peer_up_kernel(idx_ref, h2_ref, gate_ref, tab_ref, act_ref, tile_a, tile_b, ht_ref, s_ref):
    h = h2_ref[...]
    hi = h.astype(_BF16).astype(_F32)
    ht_ref[...] = jnp.concatenate([hi, h - hi], axis=0).T.astype(_BF16)
    s_ref[...] = jnp.zeros_like(s_ref)
    lane = lax.broadcasted_iota(jnp.int32, (N_PAIRS, 2 * TB_PEER), 1)

    def consume(t, rows):
        r = jnp.dot(rows, ht_ref[...], preferred_element_type=_F32)
        s_ref[...] += jnp.where((lane == t) | (lane == t + TB_PEER), r, 0.0)

    _token_pipeline(idx_ref, tab_ref, tile_a, tile_b, consume)
    st = s_ref[...].T
    s = st[:TB_PEER] + st[TB_PEER:]
    act_ref[...] = 0.5 * s * (1.0 + lax.erf(s * (2.0 ** -0.5))) * gate_ref[...]


def _peer_down_kernel(idx_ref, act_ref, x1_ref, tab_ref, g_ref, o_ref, tile_a, tile_b, y_ref):
    def consume(t, rows):
        lhs = _split_bf16(act_ref[pl.ds(t, 1), :])
        y2 = jnp.dot(lhs, rows, preferred_element_type=_F32)
        y_ref[pl.ds(t, 1), :] = y2[0:1] + y2[1:2]

    _token_pipeline(idx_ref, tab_ref, tile_a, tile_b, consume)
    o_ref[...] = _rmsnorm(x1_ref[...] + y_ref[...], g_ref[...])


def _peer_specs():
    tok = lambda w: pl.BlockSpec((TB_PEER, w), lambda i: (i, 0))
    idx = pl.BlockSpec((TB_PEER, N_PAIRS), lambda i: (i, 0), memory_space=pltpu.SMEM)
    table = pl.BlockSpec(memory_space=pltpu.VMEM)
    tile = pltpu.VMEM((ROW_VREGS * TILE_STRIDE, LANES), _F32)
    params = pltpu.CompilerParams(dimension_semantics=("arbitrary",), vmem_limit_bytes=VMEM_LIMIT_PEER)
    return tok, idx, table, tile, params


def _peer_up(idx, h2, gate, table):
    n = h2.shape[0]
    tok, idx_spec, table_spec, tile, params = _peer_specs()
    return pl.pallas_call(
        _peer_up_kernel,
        grid=(n // TB_PEER,),
        in_specs=[idx_spec, tok(D_MODEL), tok(N_PAIRS), table_spec],
        out_specs=tok(N_PAIRS),
        out_shape=jax.ShapeDtypeStruct((n, N_PAIRS), _F32),
        scratch_shapes=[tile, tile, pltpu.VMEM((D_MODEL, 2 * TB_PEER), _BF16),
                        pltpu.VMEM((N_PAIRS, 2 * TB_PEER), _F32)],
        compiler_params=params,
        name="peer_up",
    )(idx, h2, gate, table)


def _peer_down(idx, act, x1, table, g_final):
    n = x1.shape[0]
    tok, idx_spec, table_spec, tile, params = _peer_specs()
    return pl.pallas_call(
        _peer_down_kernel,
        grid=(n // TB_PEER,),
        in_specs=[idx_spec, tok(N_PAIRS), tok(D_MODEL), table_spec,
                  pl.BlockSpec(g_final.shape, lambda i: (0, 0))],
        out_specs=tok(D_MODEL),
        out_shape=jax.ShapeDtypeStruct((n, D_MODEL), _F32),
        scratch_shapes=[tile, tile, pltpu.VMEM((TB_PEER, D_MODEL), _F32)],
        compiler_params=params,
        name="peer_down",
    )(idx, act, x1, table, g_final)


def _expert_table(t):
    tb = t.astype(_BF16).reshape(N_EXPERTS, TABLE_ROWS_PER_EXPERT, 2, LANES).transpose(0, 1, 3, 2)
    return lax.bitcast_convert_type(tb, jnp.uint32).reshape(N_EXPERTS * TABLE_ROWS_PER_EXPERT, LANES)


def kernel(x, norm1_g, w_in, sink_logits, pool_w, pool_scale, w_branch, w_gate, b_gate, w_out,
           norm2_g, peer_w_query, peer_sub_keys, peer_u, peer_v, norm_final_g):
    b, s, d = x.shape
    n = b * s
    assert norm1_g.shape[0] == 1 and d == D_MODEL
    assert s % TM_MIX == 0 and s % BLOCK == 0 and n % TM_PROJ == 0 and n % TM_ROUTE == 0 and n % TB_PEER == 0
    row = lambda a: a.reshape(1, -1)
    x2 = x.reshape(n, d)
    g1 = row(norm1_g[0])
    q, k, v, u = _in_proj(x2, g1, w_in[0].astype(_BF16))
    o_attn = _attention(q.reshape(b, s, ATTN_WIDTH), k.reshape(b, s, KV_WIDTH),
                        v.reshape(b, s, KV_WIDTH), sink_logits[0]).reshape(n, ATTN_WIDTH)
    x1, h2 = _mix(x2, u, o_attn, g1, w_gate[0].astype(_BF16), row(b_gate[0]),
                  pool_w[0].astype(_BF16), row(pool_scale[0]), w_branch[0].astype(_BF16),
                  w_out[0].astype(_BF16), row(norm2_g[0]), s)
    idx, gate = _route(h2, peer_w_query[0].astype(_BF16), peer_sub_keys[0].astype(_BF16))
    act = _peer_up(idx, h2, gate, _expert_table(peer_u[0]))
    y = _peer_down(idx, act, x1, _expert_table(peer_v[0]), row(norm_final_g))
    return y.reshape(b, s, d)
```

```python
import functools

import jax
import jax.numpy as jnp
from jax import lax
from jax.experimental import pallas as pl
from jax.experimental.pallas import tpu as pltpu

D_MODEL = 1024
N_Q_HEADS = 8
N_KV_HEADS = 2
HEAD_DIM = 64
Q_GROUP = N_Q_HEADS // N_KV_HEADS
ATTN_WIDTH = N_Q_HEADS * HEAD_DIM
KV_WIDTH = N_KV_HEADS * HEAD_DIM
WINDOW = 128
BLOCK = 128
POOL_WINDOWS = (2, 4, 8, 16)
POOL_WIDTH = D_MODEL // 2
POOL_GROUP_DIM = POOL_WIDTH // len(POOL_WINDOWS)
IN_WIDTH = ATTN_WIDTH + 2 * KV_WIDTH + POOL_WIDTH
PEER_HEADS = 8
N_KEYS = 128
N_EXPERTS = N_KEYS * N_KEYS
PEER_HALF = 128
PEER_TOPK = 16
N_PAIRS = PEER_HEADS * PEER_TOPK
EPS = 1e-6
NEG_INF = -1e30

SUBLANES = 8
LANES = 128
POOL_HALO = 8
ROW_VREGS = D_MODEL // LANES
TABLE_ROWS_PER_EXPERT = ROW_VREGS // 2
TILE_STRIDE = N_PAIRS + 1

TM_PROJ = 512
TM_MIX = 256
TM_ROUTE = 512
ROUTE_TOKENS_PER_PICK = 512
TB_PEER = 128
VMEM_LIMIT_DENSE = 48 * 1024 * 1024
VMEM_LIMIT_PEER = 56 * 1024 * 1024

_F32 = jnp.float32
_BF16 = jnp.bfloat16


def _rmsnorm(x, g):
    return x * lax.rsqrt(jnp.mean(x * x, axis=-1, keepdims=True) + EPS) * g


def _split_bf16(a):
    hi = a.astype(_BF16)
    lo = (a - hi.astype(_F32)).astype(_BF16)
    return jnp.concatenate([hi, lo], axis=0)


def _in_proj_kernel(x_ref, g_ref, w_ref, q_ref, k_ref, v_ref, u_ref):
    h = _rmsnorm(x_ref[...], g_ref[...])
    proj = jnp.dot(h.astype(_BF16), w_ref[...], preferred_element_type=_F32)
    q_ref[...] = proj[:, :ATTN_WIDTH].astype(_BF16)
    k_ref[...] = proj[:, ATTN_WIDTH:ATTN_WIDTH + KV_WIDTH].astype(_BF16)
    v_ref[...] = proj[:, ATTN_WIDTH + KV_WIDTH:ATTN_WIDTH + 2 * KV_WIDTH].astype(_BF16)
    u_ref[...] = proj[:, ATTN_WIDTH + 2 * KV_WIDTH:]


def _in_proj(x2, g1, w_in):
    n = x2.shape[0]
    row = lambda w: pl.BlockSpec((TM_PROJ, w), lambda i: (i, 0))
    full = lambda a: pl.BlockSpec(a.shape, lambda i: (0,) * a.ndim)
    return pl.pallas_call(
        _in_proj_kernel,
        grid=(n // TM_PROJ,),
        in_specs=[row(D_MODEL), full(g1), full(w_in)],
        out_specs=[row(ATTN_WIDTH), row(KV_WIDTH), row(KV_WIDTH), row(POOL_WIDTH)],
        out_shape=[jax.ShapeDtypeStruct((n, ATTN_WIDTH), _BF16),
                   jax.ShapeDtypeStruct((n, KV_WIDTH), _BF16),
                   jax.ShapeDtypeStruct((n, KV_WIDTH), _BF16),
                   jax.ShapeDtypeStruct((n, POOL_WIDTH), _F32)],
        compiler_params=pltpu.CompilerParams(
            dimension_semantics=("arbitrary",), vmem_limit_bytes=VMEM_LIMIT_DENSE),
        name="in_proj",
    )(x2, g1, w_in)


def _attention_kernel(sink_ref, q_ref, kp_ref, kc_ref, kn_ref, vp_ref, vc_ref, vn_ref, o_ref, *, n_blocks):
    j = pl.program_id(1)
    rows = Q_GROUP * BLOCK
    qpos = lax.broadcasted_iota(jnp.int32, (rows, 3 * BLOCK), 0) % BLOCK
    cpos = lax.broadcasted_iota(jnp.int32, (rows, 3 * BLOCK), 1)
    grp = lax.broadcasted_iota(jnp.int32, (rows, 3 * BLOCK), 0) // BLOCK
    dist = jnp.abs(qpos + BLOCK - cpos)
    lo_c = jnp.where(j == 0, BLOCK, 0)
    hi_c = jnp.where(j == n_blocks - 1, 2 * BLOCK, 3 * BLOCK)
    valid = (dist <= WINDOW) & (cpos >= lo_c) & (cpos < hi_c)
    distf = dist.astype(_F32)
    row_grp = lax.broadcasted_iota(jnp.int32, (rows, 1), 0) // BLOCK
    q = q_ref[0]
    outs = []
    for kh in range(N_KV_HEADS):
        sl = slice(kh * HEAD_DIM, (kh + 1) * HEAD_DIM)
        kb = jnp.concatenate([kp_ref[0][:, sl], kc_ref[0][:, sl], kn_ref[0][:, sl]], axis=0)
        vb = jnp.concatenate([vp_ref[0][:, sl], vc_ref[0][:, sl], vn_ref[0][:, sl]], axis=0)
        qs = jnp.concatenate(
            [q[:, (kh * Q_GROUP + g) * HEAD_DIM:(kh * Q_GROUP + g + 1) * HEAD_DIM] for g in range(Q_GROUP)],
            axis=0)
        logits = lax.dot_general(qs, kb, (((1,), (1,)), ((), ())),
                                 preferred_element_type=_F32) * (HEAD_DIM ** -0.5)
        slope = jnp.zeros((rows, 3 * BLOCK), _F32)
        sink = jnp.zeros((rows, 1), _F32)
        for g in range(Q_GROUP):
            head = kh * Q_GROUP + g
            slope = jnp.where(grp == g, 2.0 ** (-8.0 * (head + 1) / N_Q_HEADS), slope)
            sink = jnp.where(row_grp == g, sink_ref[head], sink)
        logits = jnp.where(valid, logits - slope * distf, NEG_INF)
        m = jnp.maximum(jnp.max(logits, axis=-1, keepdims=True), sink)
        p = jnp.exp(logits - m)
        denom = jnp.sum(p, axis=-1, keepdims=True) + jnp.exp(sink - m)
        o = jnp.dot(p.astype(_BF16), vb, preferred_element_type=_F32) / denom
        outs += [o[g * BLOCK:(g + 1) * BLOCK] for g in range(Q_GROUP)]
    o_ref[0] = jnp.concatenate(outs, axis=-1).astype(_BF16)


def _attention(q3, k3, v3, sink):
    b, s, _ = q3.shape
    nb = s // BLOCK
    cur = lambda w: pl.BlockSpec((1, BLOCK, w), lambda bi, j: (bi, j, 0))
    prv = lambda w: pl.BlockSpec((1, BLOCK, w), lambda bi, j: (bi, jnp.maximum(j - 1, 0), 0))
    nxt = lambda w: pl.BlockSpec((1, BLOCK, w), lambda bi, j: (bi, jnp.minimum(j + 1, nb - 1), 0))
    return pl.pallas_call(
        functools.partial(_attention_kernel, n_blocks=nb),
        grid=(b, nb),
        in_specs=[pl.BlockSpec(memory_space=pltpu.SMEM), cur(ATTN_WIDTH),
                  prv(KV_WIDTH), cur(KV_WIDTH), nxt(KV_WIDTH),
                  prv(KV_WIDTH), cur(KV_WIDTH), nxt(KV_WIDTH)],
        out_specs=cur(ATTN_WIDTH),
        out_shape=jax.ShapeDtypeStruct((b, s, ATTN_WIDTH), _BF16),
        compiler_params=pltpu.CompilerParams(
            dimension_semantics=("arbitrary", "arbitrary"), vmem_limit_bytes=VMEM_LIMIT_DENSE),
        name="attention",
    )(sink, q3, k3, k3, k3, v3, v3, v3)


def _shift_rows(a, k):
    n = a.shape[0]
    return pltpu.roll(a, (-k) % n, axis=0)


def _mix_kernel(x_ref, u_ref, up_ref, un_ref, oa_ref, g1_ref, wg_ref, bg_ref, pw_ref, ps_ref,
                wb_ref, wo_ref, g2_ref, x1_ref, h2_ref, *, seq):
    i = pl.program_id(0)
    t0 = (i * TM_MIX) % seq
    x = x_ref[...]
    h = _rmsnorm(x, g1_ref[...]).astype(_BF16)

    prev = jnp.where(t0 > 0, up_ref[...], 0.0)
    nxt = jnp.where(t0 + TM_MIX < seq, un_ref[...], 0.0)
    ue = jnp.concatenate([prev, u_ref[...], nxt], axis=0)
    t = t0 + lax.broadcasted_iota(jnp.int32, (TM_MIX, 1), 0)
    centre = slice(POOL_HALO, POOL_HALO + TM_MIX)
    pooled = []
    for gi, w in enumerate(POOL_WINDOWS):
        ug = ue[:, gi * POOL_GROUP_DIM:(gi + 1) * POOL_GROUP_DIM]
        acc = ug + _shift_rows(ug, -1)
        half = 1
        while 2 * half < w:
            acc = _shift_rows(acc, half) + _shift_rows(acc, -half)
            half *= 2
        cnt = (jnp.minimum(t + (w - w // 2), seq) - jnp.maximum(t - w // 2, 0)).astype(_F32)
        mixed = acc[centre] / cnt - ug[centre]
        y = jnp.dot(mixed.astype(_BF16), pw_ref[gi], preferred_element_type=_F32)
        pooled.append(y)
    o_pool = jnp.concatenate(pooled, axis=-1) * ps_ref[...]

    gates = jax.nn.sigmoid(jnp.dot(h, wg_ref[...], preferred_element_type=_F32) + bg_ref[...])
    br_a = jnp.dot(oa_ref[...], wb_ref[0], preferred_element_type=_F32)
    br_b = jnp.dot(o_pool.astype(_BF16), wb_ref[1], preferred_element_type=_F32)
    merged = gates[:, :D_MODEL] * br_a + gates[:, D_MODEL:] * br_b
    x1 = x + jnp.dot(merged.astype(_BF16), wo_ref[...], preferred_element_type=_F32)
    x1_ref[...] = x1
    h2_ref[...] = _rmsnorm(x1, g2_ref[...])


def _mix(x2, u, o_attn, g1, w_gate, b_gate, pool_w, pool_scale, w_branch, w_out, g2, seq):
    n = x2.shape[0]
    per_halo = TM_MIX // POOL_HALO
    n_halo = n // POOL_HALO
    row = lambda w: pl.BlockSpec((TM_MIX, w), lambda i: (i, 0))
    full = lambda a: pl.BlockSpec(a.shape, lambda i: (0,) * a.ndim)
    halo_prev = pl.BlockSpec((POOL_HALO, POOL_WIDTH), lambda i: (jnp.maximum(i * per_halo - 1, 0), 0))
    halo_next = pl.BlockSpec((POOL_HALO, POOL_WIDTH), lambda i: (jnp.minimum((i + 1) * per_halo, n_halo - 1), 0))
    return pl.pallas_call(
        functools.partial(_mix_kernel, seq=seq),
        grid=(n // TM_MIX,),
        in_specs=[row(D_MODEL), row(POOL_WIDTH), halo_prev, halo_next, row(ATTN_WIDTH),
                  full(g1), full(w_gate), full(b_gate), full(pool_w), full(pool_scale),
                  full(w_branch), full(w_out), full(g2)],
        out_specs=[row(D_MODEL), row(D_MODEL)],
        out_shape=[jax.ShapeDtypeStruct((n, D_MODEL), _F32), jax.ShapeDtypeStruct((n, D_MODEL), _F32)],
        compiler_params=pltpu.CompilerParams(
            dimension_semantics=("arbitrary",), vmem_limit_bytes=VMEM_LIMIT_DENSE),
        name="mix",
    )(x2, u, u, u, o_attn, g1, w_gate, b_gate, pool_w, pool_scale, w_branch, w_out, g2)


def _top_rows(s, order, payload, k):
    vals, picks = [], []
    for _ in range(k):
        m = jnp.max(s, axis=0, keepdims=True)
        am = jnp.min(jnp.where(s == m, order, jnp.inf), axis=0, keepdims=True)
        hit = order == am
        vals.append(m)
        picks.append(am if payload is None else jnp.max(jnp.where(hit, payload, -1.0), axis=0, keepdims=True))
        s = jnp.where(hit, -jnp.inf, s)
    return jnp.concatenate(vals, axis=0), jnp.concatenate(picks, axis=0)


def _staircase(first, second, combine, t):
    half = PEER_TOPK // 2
    rows16 = lax.broadcasted_iota(jnp.int32, (PEER_TOPK, t), 0).astype(_F32)
    rows8 = lax.broadcasted_iota(jnp.int32, (half, t), 0).astype(_F32)
    vals = [combine(first, second[0:1])]
    pos = [rows16 * PEER_TOPK]
    ok = [rows16 >= 0]
    for b in range(1, half):
        vals.append(combine(first[0:half], second[b:b + 1]))
        pos.append(rows8 * PEER_TOPK + b)
        ok.append(rows8 < PEER_TOPK // (b + 1))
    vals.append(combine(first[0:1], second[half:]))
    pos.append(rows8 + half)
    ok.append(rows8 >= 0)
    return jnp.concatenate(vals, axis=0), jnp.concatenate(pos, axis=0), jnp.concatenate(ok, axis=0)


def _route_kernel(h2_ref, wq_ref, keys_ref, idx_ref, gate_ref, qp_ref, e_ref, g_ref):
    t = ROUTE_TOKENS_PER_PICK
    n_tiles = TM_ROUTE // t
    qp_ref[...] = jnp.dot(h2_ref[...].astype(_BF16), wq_ref[...], preferred_element_type=_F32)
    key_order = lax.broadcasted_iota(jnp.int32, (N_KEYS, t), 0).astype(_F32)

    def head_tile(i, carry):
        hd = i // n_tiles
        tok0 = pl.multiple_of((i % n_tiles) * t, t)
        tops, topi = [], []
        for p in range(2):
            c0 = pl.multiple_of((hd * 2 + p) * PEER_HALF, PEER_HALF)
            qhp = qp_ref[pl.ds(tok0, t), pl.ds(c0, PEER_HALF)].astype(_BF16)
            sc = lax.dot_general(keys_ref[hd, p], qhp, (((1,), (1,)), ((), ())),
                                 preferred_element_type=_F32)
            ts, ti = _top_rows(sc, key_order, None, PEER_TOPK)
            tops.append(ts)
            topi.append(ti)
        cand_s, cand_pos, cand_ok = _staircase(tops[0], tops[1], lambda a, b: a + b, t)
        cand_e, _, _ = _staircase(topi[0], topi[1],
                                  lambda a, b: (a * N_KEYS + b) * TABLE_ROWS_PER_EXPERT, t)
        cand_s = jnp.where(cand_ok, cand_s, -jnp.inf)
        best_s, best_e = _top_rows(cand_s, cand_pos, cand_e, PEER_TOPK)
        ex = jnp.exp(best_s - jnp.max(best_s, axis=0, keepdims=True))
        row0 = pl.multiple_of(hd * PEER_TOPK, PEER_TOPK)
        g_ref[pl.ds(row0, PEER_TOPK), pl.ds(tok0, t)] = ex / jnp.sum(ex, axis=0, keepdims=True)
        e_ref[pl.ds(row0, PEER_TOPK), pl.ds(tok0, t)] = best_e
        return carry

    lax.fori_loop(0, PEER_HEADS * n_tiles, head_tile, 0)
    idx_ref[...] = e_ref[...].T.astype(jnp.int32)
    gate_ref[...] = g_ref[...].T


def _route(h2, w_query, sub_keys):
    n = h2.shape[0]
    full = lambda a: pl.BlockSpec(a.shape, lambda i: (0,) * a.ndim)
    return pl.pallas_call(
        _route_kernel,
        grid=(n // TM_ROUTE,),
        in_specs=[pl.BlockSpec((TM_ROUTE, D_MODEL), lambda i: (i, 0)), full(w_query), full(sub_keys)],
        out_specs=[pl.BlockSpec((TM_ROUTE, N_PAIRS), lambda i: (i, 0)),
                   pl.BlockSpec((TM_ROUTE, N_PAIRS), lambda i: (i, 0))],
        out_shape=[jax.ShapeDtypeStruct((n, N_PAIRS), jnp.int32), jax.ShapeDtypeStruct((n, N_PAIRS), _F32)],
        scratch_shapes=[pltpu.VMEM((TM_ROUTE, 2 * PEER_HEADS * PEER_HALF), _F32),
                        pltpu.VMEM((N_PAIRS, TM_ROUTE), _F32),
                        pltpu.VMEM((N_PAIRS, TM_ROUTE), _F32)],
        compiler_params=pltpu.CompilerParams(
            dimension_semantics=("arbitrary",), vmem_limit_bytes=VMEM_LIMIT_DENSE),
        name="route",
    )(h2, w_query, sub_keys)


def _gather_to_tile(idx_ref, t, tab_ref, tile_ref):
    row_ref = idx_ref.at[t]
    for k in range(N_PAIRS):
        r = pl.multiple_of(row_ref[k], TABLE_ROWS_PER_EXPERT)
        slab = pltpu.bitcast(tab_ref[pl.ds(r, TABLE_ROWS_PER_EXPERT), :], _BF16)
        tile_ref[pl.ds(k, ROW_VREGS, stride=TILE_STRIDE), :] = slab.astype(_F32)


def _tile_rows(tile_ref):
    return jnp.concatenate(
        [tile_ref[pl.ds(r * TILE_STRIDE, N_PAIRS), :] for r in range(ROW_VREGS)], axis=-1).astype(_BF16)


def _token_pipeline(idx_ref, tab_ref, tile_a, tile_b, consume_pair):
    _gather_to_tile(idx_ref, 0, tab_ref, tile_a)
    _gather_to_tile(idx_ref, 1, tab_ref, tile_b)

    def pair(j, carry):
        t0 = 2 * j
        consume_pair(t0, _tile_rows(tile_a), _tile_rows(tile_b))
        _gather_to_tile(idx_ref, t0 + 2, tab_ref, tile_a)
        _gather_to_tile(idx_ref, t0 + 3, tab_ref, tile_b)
        return carry

    lax.fori_loop(0, TB_PEER // 2 - 1, pair, 0)
    consume_pair(TB_PEER - 2, _tile_rows(tile_a), _tile_rows(tile_b))


def _peer_up_kernel(idx_ref, h2_ref, gate_ref, tab_ref, act_ref, tile_a, tile_b, ht_ref, s_ref):
    h = h2_ref[...]
    hi = h.astype(_BF16).astype(_F32)
    ht_ref[...] = jnp.concatenate([hi, h - hi], axis=0).T.astype(_BF16)
    s_ref[...] = jnp.zeros_like(s_ref)
    lane = lax.broadcasted_iota(jnp.int32, (N_PAIRS, 2 * TB_PEER), 1)

    def consume_pair(t, rows_a, rows_b):
        r = jnp.dot(jnp.concatenate([rows_a, rows_b], axis=0), ht_ref[...],
                    preferred_element_type=_F32)
        own_a = (lane == t) | (lane == t + TB_PEER)
        own_b = (lane == t + 1) | (lane == t + 1 + TB_PEER)
        s_ref[...] += jnp.where(own_a, r[:N_PAIRS], 0.0) + jnp.where(own_b, r[N_PAIRS:], 0.0)

    _token_pipeline(idx_ref, tab_ref, tile_a, tile_b, consume_pair)
    st = s_ref[...].T
    s = st[:TB_PEER] + st[TB_PEER:]
    act_ref[...] = 0.5 * s * (1.0 + lax.erf(s * (2.0 ** -0.5))) * gate_ref[...]


def _peer_down_kernel(idx_ref, act_ref, x1_ref, tab_ref, g_ref, o_ref, tile_a, tile_b, y_ref):
    def consume(t, rows):
        lhs = _split_bf16(act_ref[pl.ds(t, 1), :])
        y2 = jnp.dot(lhs, rows, preferred_element_type=_F32)
        y_ref[pl.ds(t, 1), :] = y2[0:1] + y2[1:2]

    def consume_pair(t, rows_a, rows_b):
        consume(t, rows_a)
        consume(t + 1, rows_b)

    _token_pipeline(idx_ref, tab_ref, tile_a, tile_b, consume_pair)
    o_ref[...] = _rmsnorm(x1_ref[...] + y_ref[...], g_ref[...])


def _peer_specs():
    tok = lambda w: pl.BlockSpec((TB_PEER, w), lambda i: (i, 0))
    idx = pl.BlockSpec((TB_PEER, N_PAIRS), lambda i: (i, 0), memory_space=pltpu.SMEM)
    table = pl.BlockSpec(memory_space=pltpu.VMEM)
    tile = pltpu.VMEM((ROW_VREGS * TILE_STRIDE, LANES), _F32)
    params = pltpu.CompilerParams(dimension_semantics=("arbitrary",), vmem_limit_bytes=VMEM_LIMIT_PEER)
    return tok, idx, table, tile, params


def _peer_up(idx, h2, gate, table):
    n = h2.shape[0]
    tok, idx_spec, table_spec, tile, params = _peer_specs()
    return pl.pallas_call(
        _peer_up_kernel,
        grid=(n // TB_PEER,),
        in_specs=[idx_spec, tok(D_MODEL), tok(N_PAIRS), table_spec],
        out_specs=tok(N_PAIRS),
        out_shape=jax.ShapeDtypeStruct((n, N_PAIRS), _F32),
        scratch_shapes=[tile, tile, pltpu.VMEM((D_MODEL, 2 * TB_PEER), _BF16),
                        pltpu.VMEM((N_PAIRS, 2 * TB_PEER), _F32)],
        compiler_params=params,
        name="peer_up",
    )(idx, h2, gate, table)


def _peer_down(idx, act, x1, table, g_final):
    n = x1.shape[0]
    tok, idx_spec, table_spec, tile, params = _peer_specs()
    return pl.pallas_call(
        _peer_down_kernel,
        grid=(n // TB_PEER,),
        in_specs=[idx_spec, tok(N_PAIRS), tok(D_MODEL), table_spec,
                  pl.BlockSpec(g_final.shape, lambda i: (0, 0))],
        out_specs=tok(D_MODEL),
        out_shape=jax.ShapeDtypeStruct((n, D_MODEL), _F32),
        scratch_shapes=[tile, tile, pltpu.VMEM((TB_PEER, D_MODEL), _F32)],
        compiler_params=params,
        name="peer_down",
    )(idx, act, x1, table, g_final)


def _expert_table(t):
    tb = t.astype(_BF16).reshape(N_EXPERTS, TABLE_ROWS_PER_EXPERT, 2, LANES).transpose(0, 1, 3, 2)
    return lax.bitcast_convert_type(tb, jnp.uint32).reshape(N_EXPERTS * TABLE_ROWS_PER_EXPERT, LANES)


def kernel(x, norm1_g, w_in, sink_logits, pool_w, pool_scale, w_branch, w_gate, b_gate, w_out,
           norm2_g, peer_w_query, peer_sub_keys, peer_u, peer_v, norm_final_g):
    b, s, d = x.shape
    n = b * s
    assert norm1_g.shape[0] == 1 and d == D_MODEL
    assert s % TM_MIX == 0 and s % BLOCK == 0 and n % TM_PROJ == 0 and n % TM_ROUTE == 0 and n % TB_PEER == 0
    row = lambda a: a.reshape(1, -1)
    x2 = x.reshape(n, d)
    g1 = row(norm1_g[0])
    q, k, v, u = _in_proj(x2, g1, w_in[0].astype(_BF16))
    o_attn = _attention(q.reshape(b, s, ATTN_WIDTH), k.reshape(b, s, KV_WIDTH),
                        v.reshape(b, s, KV_WIDTH), sink_logits[0]).reshape(n, ATTN_WIDTH)
    x1, h2 = _mix(x2, u, o_attn, g1, w_gate[0].astype(_BF16), row(b_gate[0]),
                  pool_w[0].astype(_BF16), row(pool_scale[0]), w_branch[0].astype(_BF16),
                  w_out[0].astype(_BF16), row(norm2_g[0]), s)
    idx, gate = _route(h2, peer_w_query[0].astype(_BF16), peer_sub_keys[0].astype(_BF16))
    act = _peer_up(idx, h2, gate, _expert_table(peer_u[0]))
    y = _peer_down(idx, act, x1, _expert_table(peer_v[0]), row(norm_final_g))
    return y.reshape(b, s, d)
```

```python
import functools

import jax
import jax.numpy as jnp
from jax import lax
from jax.experimental import pallas as pl
from jax.experimental.pallas import tpu as pltpu

D_MODEL = 1024
N_Q_HEADS = 8
N_KV_HEADS = 2
HEAD_DIM = 64
Q_GROUP = N_Q_HEADS // N_KV_HEADS
ATTN_WIDTH = N_Q_HEADS * HEAD_DIM
KV_WIDTH = N_KV_HEADS * HEAD_DIM
WINDOW = 128
BLOCK = 128
POOL_WINDOWS = (2, 4, 8, 16)
POOL_WIDTH = D_MODEL // 2
POOL_GROUP_DIM = POOL_WIDTH // len(POOL_WINDOWS)
IN_WIDTH = ATTN_WIDTH + 2 * KV_WIDTH + POOL_WIDTH
PEER_HEADS = 8
N_KEYS = 128
N_EXPERTS = N_KEYS * N_KEYS
PEER_HALF = 128
PEER_TOPK = 16
N_PAIRS = PEER_HEADS * PEER_TOPK
EPS = 1e-6
NEG_INF = -1e30

SUBLANES = 8
LANES = 128
POOL_HALO = 8
ROW_VREGS = D_MODEL // LANES
TABLE_ROWS_PER_EXPERT = ROW_VREGS // 2
TILE_STRIDE = N_PAIRS + 1

TM_PROJ = 512
TM_MIX = 256
TM_ROUTE = 512
ROUTE_TOKENS_PER_PICK = 512
TB_PEER = 128
UP_GROUP = LANES // 2
VMEM_LIMIT_DENSE = 48 * 1024 * 1024
VMEM_LIMIT_PEER = 56 * 1024 * 1024

_F32 = jnp.float32
_BF16 = jnp.bfloat16


def _rmsnorm(x, g):
    return x * lax.rsqrt(jnp.mean(x * x, axis=-1, keepdims=True) + EPS) * g


def _split_bf16(a):
    hi = a.astype(_BF16)
    lo = (a - hi.astype(_F32)).astype(_BF16)
    return jnp.concatenate([hi, lo], axis=0)


def _in_proj_kernel(x_ref, g_ref, w_ref, q_ref, k_ref, v_ref, u_ref):
    h = _rmsnorm(x_ref[...], g_ref[...])
    proj = jnp.dot(h.astype(_BF16), w_ref[...], preferred_element_type=_F32)
    q_ref[...] = proj[:, :ATTN_WIDTH].astype(_BF16)
    k_ref[...] = proj[:, ATTN_WIDTH:ATTN_WIDTH + KV_WIDTH].astype(_BF16)
    v_ref[...] = proj[:, ATTN_WIDTH + KV_WIDTH:ATTN_WIDTH + 2 * KV_WIDTH].astype(_BF16)
    u_ref[...] = proj[:, ATTN_WIDTH + 2 * KV_WIDTH:]


def _in_proj(x2, g1, w_in):
    n = x2.shape[0]
    row = lambda w: pl.BlockSpec((TM_PROJ, w), lambda i: (i, 0))
    full = lambda a: pl.BlockSpec(a.shape, lambda i: (0,) * a.ndim)
    return pl.pallas_call(
        _in_proj_kernel,
        grid=(n // TM_PROJ,),
        in_specs=[row(D_MODEL), full(g1), full(w_in)],
        out_specs=[row(ATTN_WIDTH), row(KV_WIDTH), row(KV_WIDTH), row(POOL_WIDTH)],
        out_shape=[jax.ShapeDtypeStruct((n, ATTN_WIDTH), _BF16),
                   jax.ShapeDtypeStruct((n, KV_WIDTH), _BF16),
                   jax.ShapeDtypeStruct((n, KV_WIDTH), _BF16),
                   jax.ShapeDtypeStruct((n, POOL_WIDTH), _F32)],
        compiler_params=pltpu.CompilerParams(
            dimension_semantics=("arbitrary",), vmem_limit_bytes=VMEM_LIMIT_DENSE),
        name="in_proj",
    )(x2, g1, w_in)


def _attention_kernel(sink_ref, q_ref, kp_ref, kc_ref, kn_ref, vp_ref, vc_ref, vn_ref, o_ref, *, n_blocks):
    j = pl.program_id(1)
    rows = Q_GROUP * BLOCK
    qpos = lax.broadcasted_iota(jnp.int32, (rows, 3 * BLOCK), 0) % BLOCK
    cpos = lax.broadcasted_iota(jnp.int32, (rows, 3 * BLOCK), 1)
    grp = lax.broadcasted_iota(jnp.int32, (rows, 3 * BLOCK), 0) // BLOCK
    dist = jnp.abs(qpos + BLOCK - cpos)
    lo_c = jnp.where(j == 0, BLOCK, 0)
    hi_c = jnp.where(j == n_blocks - 1, 2 * BLOCK, 3 * BLOCK)
    valid = (dist <= WINDOW) & (cpos >= lo_c) & (cpos < hi_c)
    distf = dist.astype(_F32)
    row_grp = lax.broadcasted_iota(jnp.int32, (rows, 1), 0) // BLOCK
    q = q_ref[0]
    outs = []
    for kh in range(N_KV_HEADS):
        sl = slice(kh * HEAD_DIM, (kh + 1) * HEAD_DIM)
        kb = jnp.concatenate([kp_ref[0][:, sl], kc_ref[0][:, sl], kn_ref[0][:, sl]], axis=0)
        vb = jnp.concatenate([vp_ref[0][:, sl], vc_ref[0][:, sl], vn_ref[0][:, sl]], axis=0)
        qs = jnp.concatenate(
            [q[:, (kh * Q_GROUP + g) * HEAD_DIM:(kh * Q_GROUP + g + 1) * HEAD_DIM] for g in range(Q_GROUP)],
            axis=0)
        logits = lax.dot_general(qs, kb, (((1,), (1,)), ((), ())),
                                 preferred_element_type=_F32) * (HEAD_DIM ** -0.5)
        slope = jnp.zeros((rows, 3 * BLOCK), _F32)
        sink = jnp.zeros((rows, 1), _F32)
        for g in range(Q_GROUP):
            head = kh * Q_GROUP + g
            slope = jnp.where(grp == g, 2.0 ** (-8.0 * (head + 1) / N_Q_HEADS), slope)
            sink = jnp.where(row_grp == g, sink_ref[head], sink)
        logits = jnp.where(valid, logits - slope * distf, NEG_INF)
        m = jnp.maximum(jnp.max(logits, axis=-1, keepdims=True), sink)
        p = jnp.exp(logits - m)
        denom = jnp.sum(p, axis=-1, keepdims=True) + jnp.exp(sink - m)
        o = jnp.dot(p.astype(_BF16), vb, preferred_element_type=_F32) / denom
        outs += [o[g * BLOCK:(g + 1) * BLOCK] for g in range(Q_GROUP)]
    o_ref[0] = jnp.concatenate(outs, axis=-1).astype(_BF16)


def _attention(q3, k3, v3, sink):
    b, s, _ = q3.shape
    nb = s // BLOCK
    cur = lambda w: pl.BlockSpec((1, BLOCK, w), lambda bi, j: (bi, j, 0))
    prv = lambda w: pl.BlockSpec((1, BLOCK, w), lambda bi, j: (bi, jnp.maximum(j - 1, 0), 0))
    nxt = lambda w: pl.BlockSpec((1, BLOCK, w), lambda bi, j: (bi, jnp.minimum(j + 1, nb - 1), 0))
    return pl.pallas_call(
        functools.partial(_attention_kernel, n_blocks=nb),
        grid=(b, nb),
        in_specs=[pl.BlockSpec(memory_space=pltpu.SMEM), cur(ATTN_WIDTH),
                  prv(KV_WIDTH), cur(KV_WIDTH), nxt(KV_WIDTH),
                  prv(KV_WIDTH), cur(KV_WIDTH), nxt(KV_WIDTH)],
        out_specs=cur(ATTN_WIDTH),
        out_shape=jax.ShapeDtypeStruct((b, s, ATTN_WIDTH), _BF16),
        compiler_params=pltpu.CompilerParams(
            dimension_semantics=("arbitrary", "arbitrary"), vmem_limit_bytes=VMEM_LIMIT_DENSE),
        name="attention",
    )(sink, q3, k3, k3, k3, v3, v3, v3)


def _shift_rows(a, k):
    n = a.shape[0]
    return pltpu.roll(a, (-k) % n, axis=0)


def _mix_kernel(x_ref, u_ref, up_ref, un_ref, oa_ref, g1_ref, wg_ref, bg_ref, pw_ref, ps_ref,
                wb_ref, wo_ref, g2_ref, x1_ref, h2_ref, *, seq):
    i = pl.program_id(0)
    t0 = (i * TM_MIX) % seq
    x = x_ref[...]
    h = _rmsnorm(x, g1_ref[...]).astype(_BF16)

    prev = jnp.where(t0 > 0, up_ref[...], 0.0)
    nxt = jnp.where(t0 + TM_MIX < seq, un_ref[...], 0.0)
    ue = jnp.concatenate([prev, u_ref[...], nxt], axis=0)
    t = t0 + lax.broadcasted_iota(jnp.int32, (TM_MIX, 1), 0)
    centre = slice(POOL_HALO, POOL_HALO + TM_MIX)
    pooled = []
    for gi, w in enumerate(POOL_WINDOWS):
        ug = ue[:, gi * POOL_GROUP_DIM:(gi + 1) * POOL_GROUP_DIM]
        acc = ug + _shift_rows(ug, -1)
        half = 1
        while 2 * half < w:
            acc = _shift_rows(acc, half) + _shift_rows(acc, -half)
            half *= 2
        cnt = (jnp.minimum(t + (w - w // 2), seq) - jnp.maximum(t - w // 2, 0)).astype(_F32)
        mixed = acc[centre] / cnt - ug[centre]
        y = jnp.dot(mixed.astype(_BF16), pw_ref[gi], preferred_element_type=_F32)
        pooled.append(y)
    o_pool = jnp.concatenate(pooled, axis=-1) * ps_ref[...]

    gates = jax.nn.sigmoid(jnp.dot(h, wg_ref[...], preferred_element_type=_F32) + bg_ref[...])
    br_a = jnp.dot(oa_ref[...], wb_ref[0], preferred_element_type=_F32)
    br_b = jnp.dot(o_pool.astype(_BF16), wb_ref[1], preferred_element_type=_F32)
    merged = gates[:, :D_MODEL] * br_a + gates[:, D_MODEL:] * br_b
    x1 = x + jnp.dot(merged.astype(_BF16), wo_ref[...], preferred_element_type=_F32)
    x1_ref[...] = x1
    h2_ref[...] = _rmsnorm(x1, g2_ref[...])


def _mix(x2, u, o_attn, g1, w_gate, b_gate, pool_w, pool_scale, w_branch, w_out, g2, seq):
    n = x2.shape[0]
    per_halo = TM_MIX // POOL_HALO
    n_halo = n // POOL_HALO
    row = lambda w: pl.BlockSpec((TM_MIX, w), lambda i: (i, 0))
    full = lambda a: pl.BlockSpec(a.shape, lambda i: (0,) * a.ndim)
    halo_prev = pl.BlockSpec((POOL_HALO, POOL_WIDTH), lambda i: (jnp.maximum(i * per_halo - 1, 0), 0))
    halo_next = pl.BlockSpec((POOL_HALO, POOL_WIDTH), lambda i: (jnp.minimum((i + 1) * per_halo, n_halo - 1), 0))
    return pl.pallas_call(
        functools.partial(_mix_kernel, seq=seq),
        grid=(n // TM_MIX,),
        in_specs=[row(D_MODEL), row(POOL_WIDTH), halo_prev, halo_next, row(ATTN_WIDTH),
                  full(g1), full(w_gate), full(b_gate), full(pool_w), full(pool_scale),
                  full(w_branch), full(w_out), full(g2)],
        out_specs=[row(D_MODEL), row(D_MODEL)],
        out_shape=[jax.ShapeDtypeStruct((n, D_MODEL), _F32), jax.ShapeDtypeStruct((n, D_MODEL), _F32)],
        compiler_params=pltpu.CompilerParams(
            dimension_semantics=("arbitrary",), vmem_limit_bytes=VMEM_LIMIT_DENSE),
        name="mix",
    )(x2, u, u, u, o_attn, g1, w_gate, b_gate, pool_w, pool_scale, w_branch, w_out, g2)


def _top_rows(s, order, payload, k):
    vals, picks = [], []
    for _ in range(k):
        m = jnp.max(s, axis=0, keepdims=True)
        am = jnp.min(jnp.where(s == m, order, jnp.inf), axis=0, keepdims=True)
        hit = order == am
        vals.append(m)
        picks.append(am if payload is None else jnp.max(jnp.where(hit, payload, -1.0), axis=0, keepdims=True))
        s = jnp.where(hit, -jnp.inf, s)
    return jnp.concatenate(vals, axis=0), jnp.concatenate(picks, axis=0)


def _staircase(first, second, combine, t):
    half = PEER_TOPK // 2
    rows16 = lax.broadcasted_iota(jnp.int32, (PEER_TOPK, t), 0).astype(_F32)
    rows8 = lax.broadcasted_iota(jnp.int32, (half, t), 0).astype(_F32)
    vals = [combine(first, second[0:1])]
    pos = [rows16 * PEER_TOPK]
    ok = [rows16 >= 0]
    for b in range(1, half):
        vals.append(combine(first[0:half], second[b:b + 1]))
        pos.append(rows8 * PEER_TOPK + b)
        ok.append(rows8 < PEER_TOPK // (b + 1))
    vals.append(combine(first[0:1], second[half:]))
    pos.append(rows8 + half)
    ok.append(rows8 >= 0)
    return jnp.concatenate(vals, axis=0), jnp.concatenate(pos, axis=0), jnp.concatenate(ok, axis=0)


def _route_kernel(h2_ref, wq_ref, keys_ref, idx_ref, gate_ref, qp_ref, e_ref, g_ref):
    t = ROUTE_TOKENS_PER_PICK
    n_tiles = TM_ROUTE // t
    qp_ref[...] = jnp.dot(h2_ref[...].astype(_BF16), wq_ref[...], preferred_element_type=_F32)
    key_order = lax.broadcasted_iota(jnp.int32, (N_KEYS, t), 0).astype(_F32)

    def head_tile(i, carry):
        hd = i // n_tiles
        tok0 = pl.multiple_of((i % n_tiles) * t, t)
        tops, topi = [], []
        for p in range(2):
            c0 = pl.multiple_of((hd * 2 + p) * PEER_HALF, PEER_HALF)
            qhp = qp_ref[pl.ds(tok0, t), pl.ds(c0, PEER_HALF)].astype(_BF16)
            sc = lax.dot_general(keys_ref[hd, p], qhp, (((1,), (1,)), ((), ())),
                                 preferred_element_type=_F32)
            ts, ti = _top_rows(sc, key_order, None, PEER_TOPK)
            tops.append(ts)
            topi.append(ti)
        cand_s, cand_pos, cand_ok = _staircase(tops[0], tops[1], lambda a, b: a + b, t)
        cand_e, _, _ = _staircase(topi[0], topi[1],
                                  lambda a, b: (a * N_KEYS + b) * TABLE_ROWS_PER_EXPERT, t)
        cand_s = jnp.where(cand_ok, cand_s, -jnp.inf)
        best_s, best_e = _top_rows(cand_s, cand_pos, cand_e, PEER_TOPK)
        ex = jnp.exp(best_s - jnp.max(best_s, axis=0, keepdims=True))
        row0 = pl.multiple_of(hd * PEER_TOPK, PEER_TOPK)
        g_ref[pl.ds(row0, PEER_TOPK), pl.ds(tok0, t)] = ex / jnp.sum(ex, axis=0, keepdims=True)
        e_ref[pl.ds(row0, PEER_TOPK), pl.ds(tok0, t)] = best_e
        return carry

    lax.fori_loop(0, PEER_HEADS * n_tiles, head_tile, 0)
    idx_ref[...] = e_ref[...].T.astype(jnp.int32)
    gate_ref[...] = g_ref[...].T


def _route(h2, w_query, sub_keys):
    n = h2.shape[0]
    full = lambda a: pl.BlockSpec(a.shape, lambda i: (0,) * a.ndim)
    return pl.pallas_call(
        _route_kernel,
        grid=(n // TM_ROUTE,),
        in_specs=[pl.BlockSpec((TM_ROUTE, D_MODEL), lambda i: (i, 0)), full(w_query), full(sub_keys)],
        out_specs=[pl.BlockSpec((TM_ROUTE, N_PAIRS), lambda i: (i, 0)),
                   pl.BlockSpec((TM_ROUTE, N_PAIRS), lambda i: (i, 0))],
        out_shape=[jax.ShapeDtypeStruct((n, N_PAIRS), jnp.int32), jax.ShapeDtypeStruct((n, N_PAIRS), _F32)],
        scratch_shapes=[pltpu.VMEM((TM_ROUTE, 2 * PEER_HEADS * PEER_HALF), _F32),
                        pltpu.VMEM((N_PAIRS, TM_ROUTE), _F32),
                        pltpu.VMEM((N_PAIRS, TM_ROUTE), _F32)],
        compiler_params=pltpu.CompilerParams(
            dimension_semantics=("arbitrary",), vmem_limit_bytes=VMEM_LIMIT_DENSE),
        name="route",
    )(h2, w_query, sub_keys)


def _gather_to_tile(idx_ref, t, tab_ref, tile_ref):
    row_ref = idx_ref.at[t]
    for k in range(N_PAIRS):
        r = pl.multiple_of(row_ref[k], TABLE_ROWS_PER_EXPERT)
        slab = pltpu.bitcast(tab_ref[pl.ds(r, TABLE_ROWS_PER_EXPERT), :], _BF16)
        tile_ref[pl.ds(k, ROW_VREGS, stride=TILE_STRIDE), :] = slab.astype(_F32)


def _tile_rows(tile_ref):
    return jnp.concatenate(
        [tile_ref[pl.ds(r * TILE_STRIDE, N_PAIRS), :] for r in range(ROW_VREGS)], axis=-1).astype(_BF16)


def _token_pipeline(idx_ref, tab_ref, tile_a, tile_b, consume_pair):
    _gather_to_tile(idx_ref, 0, tab_ref, tile_a)
    _gather_to_tile(idx_ref, 1, tab_ref, tile_b)

    def pair(j, carry):
        t0 = 2 * j
        consume_pair(t0, _tile_rows(tile_a), _tile_rows(tile_b))
        _gather_to_tile(idx_ref, t0 + 2, tab_ref, tile_a)
        _gather_to_tile(idx_ref, t0 + 3, tab_ref, tile_b)
        return carry

    lax.fori_loop(0, TB_PEER // 2 - 1, pair, 0)
    consume_pair(TB_PEER - 2, _tile_rows(tile_a), _tile_rows(tile_b))


def _peer_up_kernel(idx_ref, h2_ref, gate_ref, tab_ref, act_ref, tile_a, tile_b, ht_ref, s_ref):
    n_groups = TB_PEER // UP_GROUP
    for g in range(n_groups):
        h = h2_ref[g * UP_GROUP:(g + 1) * UP_GROUP, :]
        hi = h.astype(_BF16).astype(_F32)
        ht_ref[g] = jnp.concatenate([hi, h - hi], axis=0).T.astype(_BF16)
    s_ref[...] = jnp.zeros_like(s_ref)
    lane = lax.broadcasted_iota(jnp.int32, (N_PAIRS, 2 * UP_GROUP), 1)

    def consume_pair(t, rows_a, rows_b):
        g = t // UP_GROUP
        c = t % UP_GROUP
        r = jnp.dot(jnp.concatenate([rows_a, rows_b], axis=0), ht_ref[g],
                    preferred_element_type=_F32)
        own_a = (lane == c) | (lane == c + UP_GROUP)
        own_b = (lane == c + 1) | (lane == c + 1 + UP_GROUP)
        s_ref[g] += jnp.where(own_a, r[:N_PAIRS], 0.0) + jnp.where(own_b, r[N_PAIRS:], 0.0)

    _token_pipeline(idx_ref, tab_ref, tile_a, tile_b, consume_pair)
    for g in range(n_groups):
        st = s_ref[g].T
        s = st[:UP_GROUP] + st[UP_GROUP:]
        rows = slice(g * UP_GROUP, (g + 1) * UP_GROUP)
        act_ref[rows, :] = 0.5 * s * (1.0 + lax.erf(s * (2.0 ** -0.5))) * gate_ref[rows, :]


def _peer_down_kernel(idx_ref, act_ref, x1_ref, tab_ref, g_ref, o_ref, tile_a, tile_b, y_ref):
    def consume(t, rows):
        lhs = _split_bf16(act_ref[pl.ds(t, 1), :])
        y2 = jnp.dot(lhs, rows, preferred_element_type=_F32)
        y_ref[pl.ds(t, 1), :] = y2[0:1] + y2[1:2]

    def consume_pair(t, rows_a, rows_b):
        consume(t, rows_a)
        consume(t + 1, rows_b)

    _token_pipeline(idx_ref, tab_ref, tile_a, tile_b, consume_pair)
    o_ref[...] = _rmsnorm(x1_ref[...] + y_ref[...], g_ref[...])


def _peer_specs():
    tok = lambda w: pl.BlockSpec((TB_PEER, w), lambda i: (i, 0))
    idx = pl.BlockSpec((TB_PEER, N_PAIRS), lambda i: (i, 0), memory_space=pltpu.SMEM)
    table = pl.BlockSpec(memory_space=pltpu.VMEM)
    tile = pltpu.VMEM((ROW_VREGS * TILE_STRIDE, LANES), _F32)
    params = pltpu.CompilerParams(dimension_semantics=("arbitrary",), vmem_limit_bytes=VMEM_LIMIT_PEER)
    return tok, idx, table, tile, params


def _peer_up(idx, h2, gate, table):
    n = h2.shape[0]
    tok, idx_spec, table_spec, tile, params = _peer_specs()
    return pl.pallas_call(
        _peer_up_kernel,
        grid=(n // TB_PEER,),
        in_specs=[idx_spec, tok(D_MODEL), tok(N_PAIRS), table_spec],
        out_specs=tok(N_PAIRS),
        out_shape=jax.ShapeDtypeStruct((n, N_PAIRS), _F32),
        scratch_shapes=[tile, tile, pltpu.VMEM((TB_PEER // UP_GROUP, D_MODEL, 2 * UP_GROUP), _BF16),
                        pltpu.VMEM((TB_PEER // UP_GROUP, N_PAIRS, 2 * UP_GROUP), _F32)],
        compiler_params=params,
        name="peer_up",
    )(idx, h2, gate, table)


def _peer_down(idx, act, x1, table, g_final):
    n = x1.shape[0]
    tok, idx_spec, table_spec, tile, params = _peer_specs()
    return pl.pallas_call(
        _peer_down_kernel,
        grid=(n // TB_PEER,),
        in_specs=[idx_spec, tok(N_PAIRS), tok(D_MODEL), table_spec,
                  pl.BlockSpec(g_final.shape, lambda i: (0, 0))],
        out_specs=tok(D_MODEL),
        out_shape=jax.ShapeDtypeStruct((n, D_MODEL), _F32),
        scratch_shapes=[tile, tile, pltpu.VMEM((TB_PEER, D_MODEL), _F32)],
        compiler_params=params,
        name="peer_down",
    )(idx, act, x1, table, g_final)


def _expert_table(t):
    tb = t.astype(_BF16).reshape(N_EXPERTS, TABLE_ROWS_PER_EXPERT, 2, LANES).transpose(0, 1, 3, 2)
    return lax.bitcast_convert_type(tb, jnp.uint32).reshape(N_EXPERTS * TABLE_ROWS_PER_EXPERT, LANES)


def kernel(x, norm1_g, w_in, sink_logits, pool_w, pool_scale, w_branch, w_gate, b_gate, w_out,
           norm2_g, peer_w_query, peer_sub_keys, peer_u, peer_v, norm_final_g):
    b, s, d = x.shape
    n = b * s
    assert norm1_g.shape[0] == 1 and d == D_MODEL
    assert s % TM_MIX == 0 and s % BLOCK == 0 and n % TM_PROJ == 0 and n % TM_ROUTE == 0 and n % TB_PEER == 0
    row = lambda a: a.reshape(1, -1)
    x2 = x.reshape(n, d)
    g1 = row(norm1_g[0])
    q, k, v, u = _in_proj(x2, g1, w_in[0].astype(_BF16))
    o_attn = _attention(q.reshape(b, s, ATTN_WIDTH), k.reshape(b, s, KV_WIDTH),
                        v.reshape(b, s, KV_WIDTH), sink_logits[0]).reshape(n, ATTN_WIDTH)
    x1, h2 = _mix(x2, u, o_attn, g1, w_gate[0].astype(_BF16), row(b_gate[0]),
                  pool_w[0].astype(_BF16), row(pool_scale[0]), w_branch[0].astype(_BF16),
                  w_out[0].astype(_BF16), row(norm2_g[0]), s)
    idx, gate = _route(h2, peer_w_query[0].astype(_BF16), peer_sub_keys[0].astype(_BF16))
    act = _peer_up(idx, h2, gate, _expert_table(peer_u[0]))
    y = _peer_down(idx, act, x1, _expert_table(peer_v[0]), row(norm_final_g))
    return y.reshape(b, s, d)
```

```python
import functools

import jax
import jax.numpy as jnp
from jax import lax
from jax.experimental import pallas as pl
from jax.experimental.pallas import tpu as pltpu

D_MODEL = 1024
N_Q_HEADS = 8
N_KV_HEADS = 2
HEAD_DIM = 64
Q_GROUP = N_Q_HEADS // N_KV_HEADS
ATTN_WIDTH = N_Q_HEADS * HEAD_DIM
KV_WIDTH = N_KV_HEADS * HEAD_DIM
WINDOW = 128
BLOCK = 128
POOL_WINDOWS = (2, 4, 8, 16)
POOL_WIDTH = D_MODEL // 2
POOL_GROUP_DIM = POOL_WIDTH // len(POOL_WINDOWS)
IN_WIDTH = ATTN_WIDTH + 2 * KV_WIDTH + POOL_WIDTH
PEER_HEADS = 8
N_KEYS = 128
N_EXPERTS = N_KEYS * N_KEYS
PEER_HALF = 128
PEER_TOPK = 16
N_PAIRS = PEER_HEADS * PEER_TOPK
EPS = 1e-6
NEG_INF = -1e30

SUBLANES = 8
LANES = 128
POOL_HALO = 8
ROW_VREGS = D_MODEL // LANES
TABLE_ROWS_PER_EXPERT = ROW_VREGS // 2
TILE_STRIDE = N_PAIRS + 1

TM_PROJ = 512
TM_MIX = 256
TM_ROUTE = 512
ROUTE_TOKENS_PER_PICK = 512
TB_PEER = 128
VMEM_LIMIT_DENSE = 48 * 1024 * 1024
VMEM_LIMIT_PEER = 56 * 1024 * 1024

_F32 = jnp.float32
_BF16 = jnp.bfloat16


def _rmsnorm(x, g):
    return x * lax.rsqrt(jnp.mean(x * x, axis=-1, keepdims=True) + EPS) * g


def _split_bf16(a):
    hi = a.astype(_BF16)
    lo = (a - hi.astype(_F32)).astype(_BF16)
    return jnp.concatenate([hi, lo], axis=0)


def _in_proj_kernel(x_ref, g_ref, w_ref, q_ref, k_ref, v_ref, u_ref):
    h = _rmsnorm(x_ref[...], g_ref[...])
    proj = jnp.dot(h.astype(_BF16), w_ref[...], preferred_element_type=_F32)
    q_ref[...] = proj[:, :ATTN_WIDTH].astype(_BF16)
    k_ref[...] = proj[:, ATTN_WIDTH:ATTN_WIDTH + KV_WIDTH].astype(_BF16)
    v_ref[...] = proj[:, ATTN_WIDTH + KV_WIDTH:ATTN_WIDTH + 2 * KV_WIDTH].astype(_BF16)
    u_ref[...] = proj[:, ATTN_WIDTH + 2 * KV_WIDTH:]


def _in_proj(x2, g1, w_in):
    n = x2.shape[0]
    row = lambda w: pl.BlockSpec((TM_PROJ, w), lambda i: (i, 0))
    full = lambda a: pl.BlockSpec(a.shape, lambda i: (0,) * a.ndim)
    return pl.pallas_call(
        _in_proj_kernel,
        grid=(n // TM_PROJ,),
        in_specs=[row(D_MODEL), full(g1), full(w_in)],
        out_specs=[row(ATTN_WIDTH), row(KV_WIDTH), row(KV_WIDTH), row(POOL_WIDTH)],
        out_shape=[jax.ShapeDtypeStruct((n, ATTN_WIDTH), _BF16),
                   jax.ShapeDtypeStruct((n, KV_WIDTH), _BF16),
                   jax.ShapeDtypeStruct((n, KV_WIDTH), _BF16),
                   jax.ShapeDtypeStruct((n, POOL_WIDTH), _F32)],
        compiler_params=pltpu.CompilerParams(
            dimension_semantics=("arbitrary",), vmem_limit_bytes=VMEM_LIMIT_DENSE),
        name="in_proj",
    )(x2, g1, w_in)


def _attention_kernel(sink_ref, q_ref, kp_ref, kc_ref, kn_ref, vp_ref, vc_ref, vn_ref, o_ref, *, n_blocks):
    j = pl.program_id(1)
    rows = Q_GROUP * BLOCK
    qpos = lax.broadcasted_iota(jnp.int32, (rows, 3 * BLOCK), 0) % BLOCK
    cpos = lax.broadcasted_iota(jnp.int32, (rows, 3 * BLOCK), 1)
    grp = lax.broadcasted_iota(jnp.int32, (rows, 3 * BLOCK), 0) // BLOCK
    dist = jnp.abs(qpos + BLOCK - cpos)
    lo_c = jnp.where(j == 0, BLOCK, 0)
    hi_c = jnp.where(j == n_blocks - 1, 2 * BLOCK, 3 * BLOCK)
    valid = (dist <= WINDOW) & (cpos >= lo_c) & (cpos < hi_c)
    distf = dist.astype(_F32)
    row_grp = lax.broadcasted_iota(jnp.int32, (rows, 1), 0) // BLOCK
    q = q_ref[0]
    outs = []
    for kh in range(N_KV_HEADS):
        sl = slice(kh * HEAD_DIM, (kh + 1) * HEAD_DIM)
        kb = jnp.concatenate([kp_ref[0][:, sl], kc_ref[0][:, sl], kn_ref[0][:, sl]], axis=0)
        vb = jnp.concatenate([vp_ref[0][:, sl], vc_ref[0][:, sl], vn_ref[0][:, sl]], axis=0)
        qs = jnp.concatenate(
            [q[:, (kh * Q_GROUP + g) * HEAD_DIM:(kh * Q_GROUP + g + 1) * HEAD_DIM] for g in range(Q_GROUP)],
            axis=0)
        logits = lax.dot_general(qs, kb, (((1,), (1,)), ((), ())),
                                 preferred_element_type=_F32) * (HEAD_DIM ** -0.5)
        slope = jnp.zeros((rows, 3 * BLOCK), _F32)
        sink = jnp.zeros((rows, 1), _F32)
        for g in range(Q_GROUP):
            head = kh * Q_GROUP + g
            slope = jnp.where(grp == g, 2.0 ** (-8.0 * (head + 1) / N_Q_HEADS), slope)
            sink = jnp.where(row_grp == g, sink_ref[head], sink)
        logits = jnp.where(valid, logits - slope * distf, NEG_INF)
        m = jnp.maximum(jnp.max(logits, axis=-1, keepdims=True), sink)
        p = jnp.exp(logits - m)
        denom = jnp.sum(p, axis=-1, keepdims=True) + jnp.exp(sink - m)
        o = jnp.dot(p.astype(_BF16), vb, preferred_element_type=_F32) / denom
        outs += [o[g * BLOCK:(g + 1) * BLOCK] for g in range(Q_GROUP)]
    o_ref[0] = jnp.concatenate(outs, axis=-1).astype(_BF16)


def _attention(q3, k3, v3, sink):
    b, s, _ = q3.shape
    nb = s // BLOCK
    cur = lambda w: pl.BlockSpec((1, BLOCK, w), lambda bi, j: (bi, j, 0))
    prv = lambda w: pl.BlockSpec((1, BLOCK, w), lambda bi, j: (bi, jnp.maximum(j - 1, 0), 0))
    nxt = lambda w: pl.BlockSpec((1, BLOCK, w), lambda bi, j: (bi, jnp.minimum(j + 1, nb - 1), 0))
    return pl.pallas_call(
        functools.partial(_attention_kernel, n_blocks=nb),
        grid=(b, nb),
        in_specs=[pl.BlockSpec(memory_space=pltpu.SMEM), cur(ATTN_WIDTH),
                  prv(KV_WIDTH), cur(KV_WIDTH), nxt(KV_WIDTH),
                  prv(KV_WIDTH), cur(KV_WIDTH), nxt(KV_WIDTH)],
        out_specs=cur(ATTN_WIDTH),
        out_shape=jax.ShapeDtypeStruct((b, s, ATTN_WIDTH), _BF16),
        compiler_params=pltpu.CompilerParams(
            dimension_semantics=("arbitrary", "arbitrary"), vmem_limit_bytes=VMEM_LIMIT_DENSE),
        name="attention",
    )(sink, q3, k3, k3, k3, v3, v3, v3)


def _shift_rows(a, k):
    n = a.shape[0]
    return pltpu.roll(a, (-k) % n, axis=0)


def _mix_kernel(x_ref, u_ref, up_ref, un_ref, oa_ref, g1_ref, wg_ref, bg_ref, pw_ref, ps_ref,
                wb_ref, wo_ref, g2_ref, x1_ref, h2_ref, h3_ref, *, seq):
    i = pl.program_id(0)
    t0 = (i * TM_MIX) % seq
    x = x_ref[...]
    h = _rmsnorm(x, g1_ref[...]).astype(_BF16)

    prev = jnp.where(t0 > 0, up_ref[...], 0.0)
    nxt = jnp.where(t0 + TM_MIX < seq, un_ref[...], 0.0)
    ue = jnp.concatenate([prev, u_ref[...], nxt], axis=0)
    t = t0 + lax.broadcasted_iota(jnp.int32, (TM_MIX, 1), 0)
    centre = slice(POOL_HALO, POOL_HALO + TM_MIX)
    pooled = []
    for gi, w in enumerate(POOL_WINDOWS):
        ug = ue[:, gi * POOL_GROUP_DIM:(gi + 1) * POOL_GROUP_DIM]
        acc = ug + _shift_rows(ug, -1)
        half = 1
        while 2 * half < w:
            acc = _shift_rows(acc, half) + _shift_rows(acc, -half)
            half *= 2
        cnt = (jnp.minimum(t + (w - w // 2), seq) - jnp.maximum(t - w // 2, 0)).astype(_F32)
        mixed = acc[centre] / cnt - ug[centre]
        y = jnp.dot(mixed.astype(_BF16), pw_ref[gi], preferred_element_type=_F32)
        pooled.append(y)
    o_pool = jnp.concatenate(pooled, axis=-1) * ps_ref[...]

    gates = jax.nn.sigmoid(jnp.dot(h, wg_ref[...], preferred_element_type=_F32) + bg_ref[...])
    br_a = jnp.dot(oa_ref[...], wb_ref[0], preferred_element_type=_F32)
    br_b = jnp.dot(o_pool.astype(_BF16), wb_ref[1], preferred_element_type=_F32)
    merged = gates[:, :D_MODEL] * br_a + gates[:, D_MODEL:] * br_b
    x1 = x + jnp.dot(merged.astype(_BF16), wo_ref[...], preferred_element_type=_F32)
    x1_ref[...] = x1
    h2 = _rmsnorm(x1, g2_ref[...])
    h2_ref[...] = h2.astype(_BF16)
    for r in range(ROW_VREGS):
        h3_ref[:, r, :] = h2[:, r * LANES:(r + 1) * LANES]


def _mix(x2, u, o_attn, g1, w_gate, b_gate, pool_w, pool_scale, w_branch, w_out, g2, seq):
    n = x2.shape[0]
    per_halo = TM_MIX // POOL_HALO
    n_halo = n // POOL_HALO
    row = lambda w: pl.BlockSpec((TM_MIX, w), lambda i: (i, 0))
    full = lambda a: pl.BlockSpec(a.shape, lambda i: (0,) * a.ndim)
    halo_prev = pl.BlockSpec((POOL_HALO, POOL_WIDTH), lambda i: (jnp.maximum(i * per_halo - 1, 0), 0))
    halo_next = pl.BlockSpec((POOL_HALO, POOL_WIDTH), lambda i: (jnp.minimum((i + 1) * per_halo, n_halo - 1), 0))
    return pl.pallas_call(
        functools.partial(_mix_kernel, seq=seq),
        grid=(n // TM_MIX,),
        in_specs=[row(D_MODEL), row(POOL_WIDTH), halo_prev, halo_next, row(ATTN_WIDTH),
                  full(g1), full(w_gate), full(b_gate), full(pool_w), full(pool_scale),
                  full(w_branch), full(w_out), full(g2)],
        out_specs=[row(D_MODEL), row(D_MODEL),
                   pl.BlockSpec((TM_MIX, ROW_VREGS, LANES), lambda i: (i, 0, 0))],
        out_shape=[jax.ShapeDtypeStruct((n, D_MODEL), _F32), jax.ShapeDtypeStruct((n, D_MODEL), _BF16),
                   jax.ShapeDtypeStruct((n, ROW_VREGS, LANES), _F32)],
        compiler_params=pltpu.CompilerParams(
            dimension_semantics=("arbitrary",), vmem_limit_bytes=VMEM_LIMIT_DENSE),
        name="mix",
    )(x2, u, u, u, o_attn, g1, w_gate, b_gate, pool_w, pool_scale, w_branch, w_out, g2)


def _top_rows(s, order, payload, k):
    vals, picks = [], []
    for _ in range(k):
        m = jnp.max(s, axis=0, keepdims=True)
        am = jnp.min(jnp.where(s == m, order, jnp.inf), axis=0, keepdims=True)
        hit = order == am
        vals.append(m)
        picks.append(am if payload is None else jnp.max(jnp.where(hit, payload, -1.0), axis=0, keepdims=True))
        s = jnp.where(hit, -jnp.inf, s)
    return jnp.concatenate(vals, axis=0), jnp.concatenate(picks, axis=0)


def _staircase(first, second, combine, t):
    half = PEER_TOPK // 2
    rows16 = lax.broadcasted_iota(jnp.int32, (PEER_TOPK, t), 0).astype(_F32)
    rows8 = lax.broadcasted_iota(jnp.int32, (half, t), 0).astype(_F32)
    vals = [combine(first, second[0:1])]
    pos = [rows16 * PEER_TOPK]
    ok = [rows16 >= 0]
    for b in range(1, half):
        vals.append(combine(first[0:half], second[b:b + 1]))
        pos.append(rows8 * PEER_TOPK + b)
        ok.append(rows8 < PEER_TOPK // (b + 1))
    vals.append(combine(first[0:1], second[half:]))
    pos.append(rows8 + half)
    ok.append(rows8 >= 0)
    return jnp.concatenate(vals, axis=0), jnp.concatenate(pos, axis=0), jnp.concatenate(ok, axis=0)


def _route_kernel(h2_ref, wq_ref, keys_ref, idx_ref, gate_ref, qp_ref, e_ref, g_ref):
    t = ROUTE_TOKENS_PER_PICK
    n_tiles = TM_ROUTE // t
    qp_ref[...] = jnp.dot(h2_ref[...], wq_ref[...], preferred_element_type=_F32)
    key_order = lax.broadcasted_iota(jnp.int32, (N_KEYS, t), 0).astype(_F32)

    def head_tile(i, carry):
        hd = i // n_tiles
        tok0 = pl.multiple_of((i % n_tiles) * t, t)
        tops, topi = [], []
        for p in range(2):
            c0 = pl.multiple_of((hd * 2 + p) * PEER_HALF, PEER_HALF)
            qhp = qp_ref[pl.ds(tok0, t), pl.ds(c0, PEER_HALF)].astype(_BF16)
            sc = lax.dot_general(keys_ref[hd, p], qhp, (((1,), (1,)), ((), ())),
                                 preferred_element_type=_F32)
            ts, ti = _top_rows(sc, key_order, None, PEER_TOPK)
            tops.append(ts)
            topi.append(ti)
        cand_s, cand_pos, cand_ok = _staircase(tops[0], tops[1], lambda a, b: a + b, t)
        cand_e, _, _ = _staircase(topi[0], topi[1],
                                  lambda a, b: (a * N_KEYS + b) * TABLE_ROWS_PER_EXPERT, t)
        cand_s = jnp.where(cand_ok, cand_s, -jnp.inf)
        best_s, best_e = _top_rows(cand_s, cand_pos, cand_e, PEER_TOPK)
        ex = jnp.exp(best_s - jnp.max(best_s, axis=0, keepdims=True))
        row0 = pl.multiple_of(hd * PEER_TOPK, PEER_TOPK)
        g_ref[pl.ds(row0, PEER_TOPK), pl.ds(tok0, t)] = ex / jnp.sum(ex, axis=0, keepdims=True)
        e_ref[pl.ds(row0, PEER_TOPK), pl.ds(tok0, t)] = best_e
        return carry

    lax.fori_loop(0, PEER_HEADS * n_tiles, head_tile, 0)
    idx_ref[...] = e_ref[...].T.astype(jnp.int32)
    gate_ref[...] = g_ref[...].T


def _route(h2, w_query, sub_keys):
    n = h2.shape[0]
    full = lambda a: pl.BlockSpec(a.shape, lambda i: (0,) * a.ndim)
    return pl.pallas_call(
        _route_kernel,
        grid=(n // TM_ROUTE,),
        in_specs=[pl.BlockSpec((TM_ROUTE, D_MODEL), lambda i: (i, 0)), full(w_query), full(sub_keys)],
        out_specs=[pl.BlockSpec((TM_ROUTE, N_PAIRS), lambda i: (i, 0)),
                   pl.BlockSpec((TM_ROUTE, N_PAIRS), lambda i: (i, 0))],
        out_shape=[jax.ShapeDtypeStruct((n, N_PAIRS), jnp.int32), jax.ShapeDtypeStruct((n, N_PAIRS), _F32)],
        scratch_shapes=[pltpu.VMEM((TM_ROUTE, 2 * PEER_HEADS * PEER_HALF), _F32),
                        pltpu.VMEM((N_PAIRS, TM_ROUTE), _F32),
                        pltpu.VMEM((N_PAIRS, TM_ROUTE), _F32)],
        compiler_params=pltpu.CompilerParams(
            dimension_semantics=("arbitrary",), vmem_limit_bytes=VMEM_LIMIT_DENSE),
        name="route",
    )(h2, w_query, sub_keys)


def _gather_to_tile(idx_ref, t, tab_ref, tile_ref):
    row_ref = idx_ref.at[t]
    for k in range(N_PAIRS):
        r = pl.multiple_of(row_ref[k], TABLE_ROWS_PER_EXPERT)
        slab = pltpu.bitcast(tab_ref[pl.ds(r, TABLE_ROWS_PER_EXPERT), :], _BF16)
        tile_ref[pl.ds(k, ROW_VREGS, stride=TILE_STRIDE), :] = slab.astype(_F32)


def _tile_rows(tile_ref):
    return jnp.concatenate(
        [tile_ref[pl.ds(r * TILE_STRIDE, N_PAIRS), :] for r in range(ROW_VREGS)], axis=-1).astype(_BF16)


def _token_pipeline(idx_ref, tab_ref, tile_a, tile_b, consume_pair):
    _gather_to_tile(idx_ref, 0, tab_ref, tile_a)
    _gather_to_tile(idx_ref, 1, tab_ref, tile_b)

    def pair(j, carry):
        t0 = 2 * j
        consume_pair(t0, _tile_rows(tile_a), _tile_rows(tile_b))
        _gather_to_tile(idx_ref, t0 + 2, tab_ref, tile_a)
        _gather_to_tile(idx_ref, t0 + 3, tab_ref, tile_b)
        return carry

    lax.fori_loop(0, TB_PEER // 2 - 1, pair, 0)
    consume_pair(TB_PEER - 2, _tile_rows(tile_a), _tile_rows(tile_b))


_FOLD_ROW_OF_SLOT = (0, 4, 2, 6, 1, 5, 3, 7)


def _fold8(slabs, sub):
    def halve(pairs, step, keep):
        out = []
        for a, b in pairs:
            if step == SUBLANES // 2:
                out.append(jnp.where(keep, a, b) + pltpu.roll(jnp.where(keep, b, a), step, 0))
            else:
                out.append(jnp.where(keep, a, pltpu.roll(b, step, 0))
                           + jnp.where(keep, pltpu.roll(a, SUBLANES - step, 0), b))
        return out

    c = halve([(slabs[2 * i], slabs[2 * i + 1]) for i in range(4)], 4, sub < 4)
    e = halve([(c[0], c[1]), (c[2], c[3])], 2, (sub % 4) < 2)
    return halve([(e[0], e[1])], 1, (sub % 2) == 0)[0]


def _peer_up_kernel(idx_ref, h3_ref, gate_ref, tab_ref, act_ref, pend_ref, s_ref):
    sub = lax.broadcasted_iota(jnp.int32, (SUBLANES, LANES), 0)

    def fold_token(t):
        h = h3_ref[t]
        row_ref = idx_ref.at[t]
        folded = []
        for grp in range(N_PAIRS // SUBLANES):
            prods = []
            for slot in range(SUBLANES):
                k = grp * SUBLANES + _FOLD_ROW_OF_SLOT[slot]
                r = pl.multiple_of(row_ref[k], TABLE_ROWS_PER_EXPERT)
                slab = pltpu.bitcast(tab_ref[pl.ds(r, TABLE_ROWS_PER_EXPERT), :], _BF16)
                prods.append(slab.astype(_F32) * h)
            folded.append(_fold8(prods, sub))
        return jnp.concatenate(folded, axis=0)

    def finish(slot, t):
        s_ref[pl.ds(t, 1), :] = jnp.sum(pend_ref[slot].T, axis=0, keepdims=True)

    pend_ref[...] = jnp.zeros_like(pend_ref)

    def pair(j, carry):
        finish(0, jnp.maximum(2 * j - 2, 0))
        finish(1, jnp.maximum(2 * j - 1, 0))
        pend_ref[0] = fold_token(2 * j)
        pend_ref[1] = fold_token(2 * j + 1)
        return carry

    lax.fori_loop(0, TB_PEER // 2, pair, 0)
    finish(0, TB_PEER - 2)
    finish(1, TB_PEER - 1)
    s = s_ref[...]
    act_ref[...] = 0.5 * s * (1.0 + lax.erf(s * (2.0 ** -0.5))) * gate_ref[...]


def _peer_down_kernel(idx_ref, act_ref, x1_ref, tab_ref, g_ref, o_ref, tile_a, tile_b, y_ref):
    def consume(t, rows):
        lhs = _split_bf16(act_ref[pl.ds(t, 1), :])
        y2 = jnp.dot(lhs, rows, preferred_element_type=_F32)
        y_ref[pl.ds(t, 1), :] = y2[0:1] + y2[1:2]

    def consume_pair(t, rows_a, rows_b):
        consume(t, rows_a)
        consume(t + 1, rows_b)

    _token_pipeline(idx_ref, tab_ref, tile_a, tile_b, consume_pair)
    o_ref[...] = _rmsnorm(x1_ref[...] + y_ref[...], g_ref[...])


def _peer_specs():
    tok = lambda w: pl.BlockSpec((TB_PEER, w), lambda i: (i, 0))
    idx = pl.BlockSpec((TB_PEER, N_PAIRS), lambda i: (i, 0), memory_space=pltpu.SMEM)
    table = pl.BlockSpec(memory_space=pltpu.VMEM)
    tile = pltpu.VMEM((ROW_VREGS * TILE_STRIDE, LANES), _F32)
    params = pltpu.CompilerParams(dimension_semantics=("arbitrary",), vmem_limit_bytes=VMEM_LIMIT_PEER)
    return tok, idx, table, tile, params


def _peer_up(idx, h3, gate, table):
    n = h3.shape[0]
    tok, idx_spec, table_spec, _, params = _peer_specs()
    return pl.pallas_call(
        _peer_up_kernel,
        grid=(n // TB_PEER,),
        in_specs=[idx_spec, pl.BlockSpec((TB_PEER, ROW_VREGS, LANES), lambda i: (i, 0, 0)),
                  tok(N_PAIRS), table_spec],
        out_specs=tok(N_PAIRS),
        out_shape=jax.ShapeDtypeStruct((n, N_PAIRS), _F32),
        scratch_shapes=[pltpu.VMEM((2, N_PAIRS, LANES), _F32), pltpu.VMEM((TB_PEER, N_PAIRS), _F32)],
        compiler_params=params,
        name="peer_up",
    )(idx, h3, gate, table)


def _peer_down(idx, act, x1, table, g_final):
    n = x1.shape[0]
    tok, idx_spec, table_spec, tile, params = _peer_specs()
    return pl.pallas_call(
        _peer_down_kernel,
        grid=(n // TB_PEER,),
        in_specs=[idx_spec, tok(N_PAIRS), tok(D_MODEL), table_spec,
                  pl.BlockSpec(g_final.shape, lambda i: (0, 0))],
        out_specs=tok(D_MODEL),
        out_shape=jax.ShapeDtypeStruct((n, D_MODEL), _F32),
        scratch_shapes=[tile, tile, pltpu.VMEM((TB_PEER, D_MODEL), _F32)],
        compiler_params=params,
        name="peer_down",
    )(idx, act, x1, table, g_final)


def _expert_table(t):
    tb = t.astype(_BF16).reshape(N_EXPERTS, TABLE_ROWS_PER_EXPERT, 2, LANES).transpose(0, 1, 3, 2)
    return lax.bitcast_convert_type(tb, jnp.uint32).reshape(N_EXPERTS * TABLE_ROWS_PER_EXPERT, LANES)


def kernel(x, norm1_g, w_in, sink_logits, pool_w, pool_scale, w_branch, w_gate, b_gate, w_out,
           norm2_g, peer_w_query, peer_sub_keys, peer_u, peer_v, norm_final_g):
    b, s, d = x.shape
    n = b * s
    assert norm1_g.shape[0] == 1 and d == D_MODEL
    assert s % TM_MIX == 0 and s % BLOCK == 0 and n % TM_PROJ == 0 and n % TM_ROUTE == 0 and n % TB_PEER == 0
    row = lambda a: a.reshape(1, -1)
    x2 = x.reshape(n, d)
    g1 = row(norm1_g[0])
    q, k, v, u = _in_proj(x2, g1, w_in[0].astype(_BF16))
    o_attn = _attention(q.reshape(b, s, ATTN_WIDTH), k.reshape(b, s, KV_WIDTH),
                        v.reshape(b, s, KV_WIDTH), sink_logits[0]).reshape(n, ATTN_WIDTH)
    x1, h2, h3 = _mix(x2, u, o_attn, g1, w_gate[0].astype(_BF16), row(b_gate[0]),
                  pool_w[0].astype(_BF16), row(pool_scale[0]), w_branch[0].astype(_BF16),
                  w_out[0].astype(_BF16), row(norm2_g[0]), s)
    idx, gate = _route(h2, peer_w_query[0].astype(_BF16), peer_sub_keys[0].astype(_BF16))
    act = _peer_up(idx, h3, gate, _expert_table(peer_u[0]))
    y = _peer_down(idx, act, x1, _expert_table(peer_v[0]), row(norm_final_g))
    return y.reshape(b, s, d)
```

```python
import functools

import jax
import jax.numpy as jnp
from jax import lax
from jax.experimental import pallas as pl
from jax.experimental.pallas import tpu as pltpu

D_MODEL = 1024
N_Q_HEADS = 8
N_KV_HEADS = 2
HEAD_DIM = 64
Q_GROUP = N_Q_HEADS // N_KV_HEADS
ATTN_WIDTH = N_Q_HEADS * HEAD_DIM
KV_WIDTH = N_KV_HEADS * HEAD_DIM
WINDOW = 128
BLOCK = 128
POOL_WINDOWS = (2, 4, 8, 16)
POOL_WIDTH = D_MODEL // 2
POOL_GROUP_DIM = POOL_WIDTH // len(POOL_WINDOWS)
IN_WIDTH = ATTN_WIDTH + 2 * KV_WIDTH + POOL_WIDTH
PEER_HEADS = 8
N_KEYS = 128
N_EXPERTS = N_KEYS * N_KEYS
PEER_HALF = 128
PEER_TOPK = 16
N_PAIRS = PEER_HEADS * PEER_TOPK
EPS = 1e-6
NEG_INF = -1e30

SUBLANES = 8
LANES = 128
POOL_HALO = 8
ROW_VREGS = D_MODEL // LANES
TABLE_ROWS_PER_EXPERT = ROW_VREGS // 2
TILE_STRIDE = N_PAIRS + 1

TM_PROJ = 512
TM_MIX = 256
TM_ROUTE = 512
ROUTE_TOKENS_PER_PICK = 512
TB_PEER = 128
TE_PACK = 512
VMEM_LIMIT_DENSE = 48 * 1024 * 1024
VMEM_LIMIT_PEER = 56 * 1024 * 1024

_F32 = jnp.float32
_BF16 = jnp.bfloat16


def _rmsnorm(x, g):
    return x * lax.rsqrt(jnp.mean(x * x, axis=-1, keepdims=True) + EPS) * g


def _in_proj_kernel(x_ref, g_ref, w_ref, q_ref, k_ref, v_ref, u_ref):
    h = _rmsnorm(x_ref[...], g_ref[...])
    proj = jnp.dot(h.astype(_BF16), w_ref[...], preferred_element_type=_F32)
    q_ref[...] = proj[:, :ATTN_WIDTH].astype(_BF16)
    k_ref[...] = proj[:, ATTN_WIDTH:ATTN_WIDTH + KV_WIDTH].astype(_BF16)
    v_ref[...] = proj[:, ATTN_WIDTH + KV_WIDTH:ATTN_WIDTH + 2 * KV_WIDTH].astype(_BF16)
    u_ref[...] = proj[:, ATTN_WIDTH + 2 * KV_WIDTH:]


def _in_proj(x2, g1, w_in):
    n = x2.shape[0]
    row = lambda w: pl.BlockSpec((TM_PROJ, w), lambda i: (i, 0))
    full = lambda a: pl.BlockSpec(a.shape, lambda i: (0,) * a.ndim)
    return pl.pallas_call(
        _in_proj_kernel,
        grid=(n // TM_PROJ,),
        in_specs=[row(D_MODEL), full(g1), full(w_in)],
        out_specs=[row(ATTN_WIDTH), row(KV_WIDTH), row(KV_WIDTH), row(POOL_WIDTH)],
        out_shape=[jax.ShapeDtypeStruct((n, ATTN_WIDTH), _BF16),
                   jax.ShapeDtypeStruct((n, KV_WIDTH), _BF16),
                   jax.ShapeDtypeStruct((n, KV_WIDTH), _BF16),
                   jax.ShapeDtypeStruct((n, POOL_WIDTH), _F32)],
        compiler_params=pltpu.CompilerParams(
            dimension_semantics=("arbitrary",), vmem_limit_bytes=VMEM_LIMIT_DENSE),
        name="in_proj",
    )(x2, g1, w_in)


def _attention_kernel(sink_ref, q_ref, kp_ref, kc_ref, kn_ref, vp_ref, vc_ref, vn_ref, o_ref, *, n_blocks):
    j = pl.program_id(1)
    rows = Q_GROUP * BLOCK
    qpos = lax.broadcasted_iota(jnp.int32, (rows, 3 * BLOCK), 0) % BLOCK
    cpos = lax.broadcasted_iota(jnp.int32, (rows, 3 * BLOCK), 1)
    grp = lax.broadcasted_iota(jnp.int32, (rows, 3 * BLOCK), 0) // BLOCK
    dist = jnp.abs(qpos + BLOCK - cpos)
    lo_c = jnp.where(j == 0, BLOCK, 0)
    hi_c = jnp.where(j == n_blocks - 1, 2 * BLOCK, 3 * BLOCK)
    valid = (dist <= WINDOW) & (cpos >= lo_c) & (cpos < hi_c)
    distf = dist.astype(_F32)
    row_grp = lax.broadcasted_iota(jnp.int32, (rows, 1), 0) // BLOCK
    q = q_ref[0]
    outs = []
    for kh in range(N_KV_HEADS):
        sl = slice(kh * HEAD_DIM, (kh + 1) * HEAD_DIM)
        kb = jnp.concatenate([kp_ref[0][:, sl], kc_ref[0][:, sl], kn_ref[0][:, sl]], axis=0)
        vb = jnp.concatenate([vp_ref[0][:, sl], vc_ref[0][:, sl], vn_ref[0][:, sl]], axis=0)
        qs = jnp.concatenate(
            [q[:, (kh * Q_GROUP + g) * HEAD_DIM:(kh * Q_GROUP + g + 1) * HEAD_DIM] for g in range(Q_GROUP)],
            axis=0)
        logits = lax.dot_general(qs, kb, (((1,), (1,)), ((), ())),
                                 preferred_element_type=_F32) * (HEAD_DIM ** -0.5)
        slope = jnp.zeros((rows, 3 * BLOCK), _F32)
        sink = jnp.zeros((rows, 1), _F32)
        for g in range(Q_GROUP):
            head = kh * Q_GROUP + g
            slope = jnp.where(grp == g, 2.0 ** (-8.0 * (head + 1) / N_Q_HEADS), slope)
            sink = jnp.where(row_grp == g, sink_ref[head], sink)
        logits = jnp.where(valid, logits - slope * distf, NEG_INF)
        m = jnp.maximum(jnp.max(logits, axis=-1, keepdims=True), sink)
        p = jnp.exp(logits - m)
        denom = jnp.sum(p, axis=-1, keepdims=True) + jnp.exp(sink - m)
        o = jnp.dot(p.astype(_BF16), vb, preferred_element_type=_F32) / denom
        outs += [o[g * BLOCK:(g + 1) * BLOCK] for g in range(Q_GROUP)]
    o_ref[0] = jnp.concatenate(outs, axis=-1).astype(_BF16)


def _attention(q3, k3, v3, sink):
    b, s, _ = q3.shape
    nb = s // BLOCK
    cur = lambda w: pl.BlockSpec((1, BLOCK, w), lambda bi, j: (bi, j, 0))
    prv = lambda w: pl.BlockSpec((1, BLOCK, w), lambda bi, j: (bi, jnp.maximum(j - 1, 0), 0))
    nxt = lambda w: pl.BlockSpec((1, BLOCK, w), lambda bi, j: (bi, jnp.minimum(j + 1, nb - 1), 0))
    return pl.pallas_call(
        functools.partial(_attention_kernel, n_blocks=nb),
        grid=(b, nb),
        in_specs=[pl.BlockSpec(memory_space=pltpu.SMEM), cur(ATTN_WIDTH),
                  prv(KV_WIDTH), cur(KV_WIDTH), nxt(KV_WIDTH),
                  prv(KV_WIDTH), cur(KV_WIDTH), nxt(KV_WIDTH)],
        out_specs=cur(ATTN_WIDTH),
        out_shape=jax.ShapeDtypeStruct((b, s, ATTN_WIDTH), _BF16),
        compiler_params=pltpu.CompilerParams(
            dimension_semantics=("arbitrary", "arbitrary"), vmem_limit_bytes=VMEM_LIMIT_DENSE),
        name="attention",
    )(sink, q3, k3, k3, k3, v3, v3, v3)


def _shift_rows(a, k):
    n = a.shape[0]
    return pltpu.roll(a, (-k) % n, axis=0)


def _mix_kernel(x_ref, u_ref, up_ref, un_ref, oa_ref, g1_ref, wg_ref, bg_ref, pw_ref, ps_ref,
                wb_ref, wo_ref, g2_ref, x1_ref, h2_ref, h3_ref, *, seq):
    i = pl.program_id(0)
    t0 = (i * TM_MIX) % seq
    x = x_ref[...]
    h = _rmsnorm(x, g1_ref[...]).astype(_BF16)

    prev = jnp.where(t0 > 0, up_ref[...], 0.0)
    nxt = jnp.where(t0 + TM_MIX < seq, un_ref[...], 0.0)
    ue = jnp.concatenate([prev, u_ref[...], nxt], axis=0)
    t = t0 + lax.broadcasted_iota(jnp.int32, (TM_MIX, 1), 0)
    centre = slice(POOL_HALO, POOL_HALO + TM_MIX)
    pooled = []
    for gi, w in enumerate(POOL_WINDOWS):
        ug = ue[:, gi * POOL_GROUP_DIM:(gi + 1) * POOL_GROUP_DIM]
        acc = ug + _shift_rows(ug, -1)
        half = 1
        while 2 * half < w:
            acc = _shift_rows(acc, half) + _shift_rows(acc, -half)
            half *= 2
        cnt = (jnp.minimum(t + (w - w // 2), seq) - jnp.maximum(t - w // 2, 0)).astype(_F32)
        mixed = acc[centre] / cnt - ug[centre]
        y = jnp.dot(mixed.astype(_BF16), pw_ref[gi], preferred_element_type=_F32)
        pooled.append(y)
    o_pool = jnp.concatenate(pooled, axis=-1) * ps_ref[...]

    gates = jax.nn.sigmoid(jnp.dot(h, wg_ref[...], preferred_element_type=_F32) + bg_ref[...])
    br_a = jnp.dot(oa_ref[...], wb_ref[0], preferred_element_type=_F32)
    br_b = jnp.dot(o_pool.astype(_BF16), wb_ref[1], preferred_element_type=_F32)
    merged = gates[:, :D_MODEL] * br_a + gates[:, D_MODEL:] * br_b
    x1 = x + jnp.dot(merged.astype(_BF16), wo_ref[...], preferred_element_type=_F32)
    x1_ref[...] = x1
    h2 = _rmsnorm(x1, g2_ref[...])
    h2_ref[...] = h2.astype(_BF16)
    for r in range(ROW_VREGS):
        h3_ref[:, r, :] = h2[:, r * LANES:(r + 1) * LANES]


def _mix(x2, u, o_attn, g1, w_gate, b_gate, pool_w, pool_scale, w_branch, w_out, g2, seq):
    n = x2.shape[0]
    per_halo = TM_MIX // POOL_HALO
    n_halo = n // POOL_HALO
    row = lambda w: pl.BlockSpec((TM_MIX, w), lambda i: (i, 0))
    full = lambda a: pl.BlockSpec(a.shape, lambda i: (0,) * a.ndim)
    halo_prev = pl.BlockSpec((POOL_HALO, POOL_WIDTH), lambda i: (jnp.maximum(i * per_halo - 1, 0), 0))
    halo_next = pl.BlockSpec((POOL_HALO, POOL_WIDTH), lambda i: (jnp.minimum((i + 1) * per_halo, n_halo - 1), 0))
    return pl.pallas_call(
        functools.partial(_mix_kernel, seq=seq),
        grid=(n // TM_MIX,),
        in_specs=[row(D_MODEL), row(POOL_WIDTH), halo_prev, halo_next, row(ATTN_WIDTH),
                  full(g1), full(w_gate), full(b_gate), full(pool_w), full(pool_scale),
                  full(w_branch), full(w_out), full(g2)],
        out_specs=[row(D_MODEL), row(D_MODEL),
                   pl.BlockSpec((TM_MIX, ROW_VREGS, LANES), lambda i: (i, 0, 0))],
        out_shape=[jax.ShapeDtypeStruct((n, D_MODEL), _F32), jax.ShapeDtypeStruct((n, D_MODEL), _BF16),
                   jax.ShapeDtypeStruct((n, ROW_VREGS, LANES), _F32)],
        compiler_params=pltpu.CompilerParams(
            dimension_semantics=("arbitrary",), vmem_limit_bytes=VMEM_LIMIT_DENSE),
        name="mix",
    )(x2, u, u, u, o_attn, g1, w_gate, b_gate, pool_w, pool_scale, w_branch, w_out, g2)


def _top_rows(s, order, payload, k):
    vals, picks = [], []
    for _ in range(k):
        m = jnp.max(s, axis=0, keepdims=True)
        am = jnp.min(jnp.where(s == m, order, jnp.inf), axis=0, keepdims=True)
        hit = order == am
        vals.append(m)
        picks.append(am if payload is None else jnp.max(jnp.where(hit, payload, -1.0), axis=0, keepdims=True))
        s = jnp.where(hit, -jnp.inf, s)
    return jnp.concatenate(vals, axis=0), jnp.concatenate(picks, axis=0)


def _staircase(first, second, combine, t):
    half = PEER_TOPK // 2
    rows16 = lax.broadcasted_iota(jnp.int32, (PEER_TOPK, t), 0).astype(_F32)
    rows8 = lax.broadcasted_iota(jnp.int32, (half, t), 0).astype(_F32)
    vals = [combine(first, second[0:1])]
    pos = [rows16 * PEER_TOPK]
    ok = [rows16 >= 0]
    for b in range(1, half):
        vals.append(combine(first[0:half], second[b:b + 1]))
        pos.append(rows8 * PEER_TOPK + b)
        ok.append(rows8 < PEER_TOPK // (b + 1))
    vals.append(combine(first[0:1], second[half:]))
    pos.append(rows8 + half)
    ok.append(rows8 >= 0)
    return jnp.concatenate(vals, axis=0), jnp.concatenate(pos, axis=0), jnp.concatenate(ok, axis=0)


def _route_kernel(h2_ref, wq_ref, keys_ref, idx_ref, gate_ref, qp_ref, e_ref, g_ref):
    t = ROUTE_TOKENS_PER_PICK
    n_tiles = TM_ROUTE // t
    qp_ref[...] = jnp.dot(h2_ref[...], wq_ref[...], preferred_element_type=_F32)
    key_order = lax.broadcasted_iota(jnp.int32, (N_KEYS, t), 0).astype(_F32)

    def head_tile(i, carry):
        hd = i // n_tiles
        tok0 = pl.multiple_of((i % n_tiles) * t, t)
        tops, topi = [], []
        for p in range(2):
            c0 = pl.multiple_of((hd * 2 + p) * PEER_HALF, PEER_HALF)
            qhp = qp_ref[pl.ds(tok0, t), pl.ds(c0, PEER_HALF)].astype(_BF16)
            sc = lax.dot_general(keys_ref[hd, p], qhp, (((1,), (1,)), ((), ())),
                                 preferred_element_type=_F32)
            ts, ti = _top_rows(sc, key_order, None, PEER_TOPK)
            tops.append(ts)
            topi.append(ti)
        cand_s, cand_pos, cand_ok = _staircase(tops[0], tops[1], lambda a, b: a + b, t)
        cand_e, _, _ = _staircase(topi[0], topi[1],
                                  lambda a, b: (a * N_KEYS + b) * TABLE_ROWS_PER_EXPERT, t)
        cand_s = jnp.where(cand_ok, cand_s, -jnp.inf)
        best_s, best_e = _top_rows(cand_s, cand_pos, cand_e, PEER_TOPK)
        ex = jnp.exp(best_s - jnp.max(best_s, axis=0, keepdims=True))
        row0 = pl.multiple_of(hd * PEER_TOPK, PEER_TOPK)
        g_ref[pl.ds(row0, PEER_TOPK), pl.ds(tok0, t)] = ex / jnp.sum(ex, axis=0, keepdims=True)
        e_ref[pl.ds(row0, PEER_TOPK), pl.ds(tok0, t)] = best_e
        return carry

    lax.fori_loop(0, PEER_HEADS * n_tiles, head_tile, 0)
    idx_ref[...] = e_ref[...].T.astype(jnp.int32)
    gate_ref[...] = g_ref[...].T


def _route(h2, w_query, sub_keys):
    n = h2.shape[0]
    full = lambda a: pl.BlockSpec(a.shape, lambda i: (0,) * a.ndim)
    return pl.pallas_call(
        _route_kernel,
        grid=(n // TM_ROUTE,),
        in_specs=[pl.BlockSpec((TM_ROUTE, D_MODEL), lambda i: (i, 0)), full(w_query), full(sub_keys)],
        out_specs=[pl.BlockSpec((TM_ROUTE, N_PAIRS), lambda i: (i, 0)),
                   pl.BlockSpec((TM_ROUTE, N_PAIRS), lambda i: (i, 0))],
        out_shape=[jax.ShapeDtypeStruct((n, N_PAIRS), jnp.int32), jax.ShapeDtypeStruct((n, N_PAIRS), _F32)],
        scratch_shapes=[pltpu.VMEM((TM_ROUTE, 2 * PEER_HEADS * PEER_HALF), _F32),
                        pltpu.VMEM((N_PAIRS, TM_ROUTE), _F32),
                        pltpu.VMEM((N_PAIRS, TM_ROUTE), _F32)],
        compiler_params=pltpu.CompilerParams(
            dimension_semantics=("arbitrary",), vmem_limit_bytes=VMEM_LIMIT_DENSE),
        name="route",
    )(h2, w_query, sub_keys)


def _expert_slab(tab_ref, row_ref, k):
    r = pl.multiple_of(row_ref[k], TABLE_ROWS_PER_EXPERT)
    return pltpu.bitcast(tab_ref[pl.ds(r, TABLE_ROWS_PER_EXPERT), :], _BF16).astype(_F32)


_FOLD_ROW_OF_SLOT = (0, 4, 2, 6, 1, 5, 3, 7)


def _fold8(slabs, sub):
    def halve(pairs, step, keep):
        out = []
        for a, b in pairs:
            if step == SUBLANES // 2:
                out.append(jnp.where(keep, a, b) + pltpu.roll(jnp.where(keep, b, a), step, 0))
            else:
                out.append(jnp.where(keep, a, pltpu.roll(b, step, 0))
                           + jnp.where(keep, pltpu.roll(a, SUBLANES - step, 0), b))
        return out

    c = halve([(slabs[2 * i], slabs[2 * i + 1]) for i in range(4)], 4, sub < 4)
    e = halve([(c[0], c[1]), (c[2], c[3])], 2, (sub % 4) < 2)
    return halve([(e[0], e[1])], 1, (sub % 2) == 0)[0]


def _peer_up_kernel(idx_ref, h3_ref, gate_ref, tab_ref, act_ref, pend_ref, s_ref):
    sub = lax.broadcasted_iota(jnp.int32, (SUBLANES, LANES), 0)

    def fold_token(t):
        h = h3_ref[t]
        row_ref = idx_ref.at[t]
        folded = []
        for grp in range(N_PAIRS // SUBLANES):
            prods = []
            for slot in range(SUBLANES):
                k = grp * SUBLANES + _FOLD_ROW_OF_SLOT[slot]
                prods.append(_expert_slab(tab_ref, row_ref, k) * h)
            folded.append(_fold8(prods, sub))
        return jnp.concatenate(folded, axis=0)

    def finish(slot, t):
        s_ref[pl.ds(t, 1), :] = jnp.sum(pend_ref[slot].T, axis=0, keepdims=True)

    pend_ref[...] = jnp.zeros_like(pend_ref)

    def pair(j, carry):
        finish(0, jnp.maximum(2 * j - 2, 0))
        finish(1, jnp.maximum(2 * j - 1, 0))
        pend_ref[0] = fold_token(2 * j)
        pend_ref[1] = fold_token(2 * j + 1)
        return carry

    lax.fori_loop(0, TB_PEER // 2, pair, 0)
    finish(0, TB_PEER - 2)
    finish(1, TB_PEER - 1)
    s = s_ref[...]
    act_ref[...] = 0.5 * s * (1.0 + lax.erf(s * (2.0 ** -0.5))) * gate_ref[...]


def _gather_to_tile(idx_ref, t, tab_ref, tile_ref):
    row_ref = idx_ref.at[t]
    for k in range(N_PAIRS):
        tile_ref[pl.ds(k, ROW_VREGS, stride=TILE_STRIDE), :] = _expert_slab(tab_ref, row_ref, k)


def _tile_rows(tile_ref):
    return jnp.concatenate(
        [tile_ref[pl.ds(r * TILE_STRIDE, N_PAIRS), :] for r in range(ROW_VREGS)], axis=-1).astype(_BF16)


def _split_bf16(a):
    hi = a.astype(_BF16)
    lo = (a - hi.astype(_F32)).astype(_BF16)
    return jnp.concatenate([hi, lo], axis=0)


def _peer_down_kernel(idx_ref, act_ref, x1_ref, tab_ref, g_ref, o_ref, tile_a, tile_b, y_ref):
    def consume(t, tile_ref):
        lhs = _split_bf16(act_ref[pl.ds(t, 1), :])
        y2 = jnp.dot(lhs, _tile_rows(tile_ref), preferred_element_type=_F32)
        y_ref[pl.ds(t, 1), :] = y2[0:1] + y2[1:2]

    _gather_to_tile(idx_ref, 0, tab_ref, tile_a)
    _gather_to_tile(idx_ref, 1, tab_ref, tile_b)

    def pair(j, carry):
        t0 = 2 * j
        consume(t0, tile_a)
        consume(t0 + 1, tile_b)
        _gather_to_tile(idx_ref, t0 + 2, tab_ref, tile_a)
        _gather_to_tile(idx_ref, t0 + 3, tab_ref, tile_b)
        return carry

    lax.fori_loop(0, TB_PEER // 2 - 1, pair, 0)
    consume(TB_PEER - 2, tile_a)
    consume(TB_PEER - 1, tile_b)
    o_ref[...] = _rmsnorm(x1_ref[...] + y_ref[...], g_ref[...])


def _peer_specs():
    tok = lambda w: pl.BlockSpec((TB_PEER, w), lambda i: (i, 0))
    idx = pl.BlockSpec((TB_PEER, N_PAIRS), lambda i: (i, 0), memory_space=pltpu.SMEM)
    table = pl.BlockSpec(memory_space=pltpu.VMEM)
    params = pltpu.CompilerParams(dimension_semantics=("arbitrary",), vmem_limit_bytes=VMEM_LIMIT_PEER)
    return tok, idx, table, params


def _peer_up(idx, h3, gate, table):
    n = h3.shape[0]
    tok, idx_spec, table_spec, params = _peer_specs()
    return pl.pallas_call(
        _peer_up_kernel,
        grid=(n // TB_PEER,),
        in_specs=[idx_spec, pl.BlockSpec((TB_PEER, ROW_VREGS, LANES), lambda i: (i, 0, 0)),
                  tok(N_PAIRS), table_spec],
        out_specs=tok(N_PAIRS),
        out_shape=jax.ShapeDtypeStruct((n, N_PAIRS), _F32),
        scratch_shapes=[pltpu.VMEM((2, N_PAIRS, LANES), _F32), pltpu.VMEM((TB_PEER, N_PAIRS), _F32)],
        compiler_params=params,
        name="peer_up",
    )(idx, h3, gate, table)


def _peer_down(idx, act, x1, table, g_final):
    n = x1.shape[0]
    tok, idx_spec, table_spec, params = _peer_specs()
    tile = pltpu.VMEM((ROW_VREGS * TILE_STRIDE, LANES), _F32)
    return pl.pallas_call(
        _peer_down_kernel,
        grid=(n // TB_PEER,),
        in_specs=[idx_spec, tok(N_PAIRS), tok(D_MODEL), table_spec,
                  pl.BlockSpec(g_final.shape, lambda i: (0, 0))],
        out_specs=tok(D_MODEL),
        out_shape=jax.ShapeDtypeStruct((n, D_MODEL), _F32),
        scratch_shapes=[tile, tile, pltpu.VMEM((TB_PEER, D_MODEL), _F32)],
        compiler_params=params,
        name="peer_down",
    )(idx, act, x1, table, g_final)


def _pack_table_kernel(t_ref, o_ref):
    x = t_ref[...]
    for i in range(TABLE_ROWS_PER_EXPERT):
        lo = x[:, (2 * i) * LANES:(2 * i + 1) * LANES]
        hi = x[:, (2 * i + 1) * LANES:(2 * i + 2) * LANES]
        words = pltpu.pack_elementwise([lo, hi], packed_dtype=_BF16)
        o_ref[pl.ds(i, TE_PACK, stride=TABLE_ROWS_PER_EXPERT), :] = pltpu.bitcast(words, jnp.uint32)


def _expert_table(t):
    return pl.pallas_call(
        _pack_table_kernel,
        grid=(N_EXPERTS // TE_PACK,),
        in_specs=[pl.BlockSpec((TE_PACK, D_MODEL), lambda i: (i, 0))],
        out_specs=pl.BlockSpec((TE_PACK * TABLE_ROWS_PER_EXPERT, LANES), lambda i: (i, 0)),
        out_shape=jax.ShapeDtypeStruct((N_EXPERTS * TABLE_ROWS_PER_EXPERT, LANES), jnp.uint32),
        compiler_params=pltpu.CompilerParams(
            dimension_semantics=("arbitrary",), vmem_limit_bytes=VMEM_LIMIT_DENSE),
        name="pack_table",
    )(t)


def kernel(x, norm1_g, w_in, sink_logits, pool_w, pool_scale, w_branch, w_gate, b_gate, w_out,
           norm2_g, peer_w_query, peer_sub_keys, peer_u, peer_v, norm_final_g):
    b, s, d = x.shape
    n = b * s
    assert norm1_g.shape[0] == 1 and d == D_MODEL
    assert s % TM_MIX == 0 and s % BLOCK == 0 and n % TM_PROJ == 0 and n % TM_ROUTE == 0 and n % TB_PEER == 0
    row = lambda a: a.reshape(1, -1)
    x2 = x.reshape(n, d)
    g1 = row(norm1_g[0])
    q, k, v, u = _in_proj(x2, g1, w_in[0].astype(_BF16))
    o_attn = _attention(q.reshape(b, s, ATTN_WIDTH), k.reshape(b, s, KV_WIDTH),
                        v.reshape(b, s, KV_WIDTH), sink_logits[0]).reshape(n, ATTN_WIDTH)
    x1, h2, h3 = _mix(x2, u, o_attn, g1, w_gate[0].astype(_BF16), row(b_gate[0]),
                  pool_w[0].astype(_BF16), row(pool_scale[0]), w_branch[0].astype(_BF16),
                  w_out[0].astype(_BF16), row(norm2_g[0]), s)
    idx, gate = _route(h2, peer_w_query[0].astype(_BF16), peer_sub_keys[0].astype(_BF16))
    act = _peer_up(idx, h3, gate, _expert_table(peer_u[0]))
    y = _peer_down(idx, act, x1, _expert_table(peer_v[0]), row(norm_final_g))
    return y.reshape(b, s, d)
```

```python
import functools

import jax
import jax.numpy as jnp
from jax import lax
from jax.experimental import pallas as pl
from jax.experimental.pallas import tpu as pltpu

D_MODEL = 1024
N_Q_HEADS = 8
N_KV_HEADS = 2
HEAD_DIM = 64
Q_GROUP = N_Q_HEADS // N_KV_HEADS
ATTN_WIDTH = N_Q_HEADS * HEAD_DIM
KV_WIDTH = N_KV_HEADS * HEAD_DIM
WINDOW = 128
BLOCK = 128
POOL_WINDOWS = (2, 4, 8, 16)
POOL_WIDTH = D_MODEL // 2
POOL_GROUP_DIM = POOL_WIDTH // len(POOL_WINDOWS)
IN_WIDTH = ATTN_WIDTH + 2 * KV_WIDTH + POOL_WIDTH
PEER_HEADS = 8
N_KEYS = 128
N_EXPERTS = N_KEYS * N_KEYS
PEER_HALF = 128
PEER_TOPK = 16
N_PAIRS = PEER_HEADS * PEER_TOPK
EPS = 1e-6
NEG_INF = -1e30

SUBLANES = 8
LANES = 128
POOL_HALO = 8
ROW_VREGS = D_MODEL // LANES
TABLE_ROWS_PER_EXPERT = ROW_VREGS // 2
TILE_STRIDE = N_PAIRS + 1

TM_PROJ = 512
TM_MIX = 256
TM_ROUTE = 512
ROUTE_TOKENS_PER_PICK = 512
TB_PEER = 128
TE_PACK = 512
IDX_CHUNK = 16
VMEM_LIMIT_DENSE = 48 * 1024 * 1024
VMEM_LIMIT_PEER = 56 * 1024 * 1024

_F32 = jnp.float32
_BF16 = jnp.bfloat16


def _rmsnorm(x, g):
    return x * lax.rsqrt(jnp.mean(x * x, axis=-1, keepdims=True) + EPS) * g


def _in_proj_kernel(x_ref, g_ref, w_ref, q_ref, k_ref, v_ref, u_ref):
    h = _rmsnorm(x_ref[...], g_ref[...])
    proj = jnp.dot(h.astype(_BF16), w_ref[...], preferred_element_type=_F32)
    q_ref[...] = proj[:, :ATTN_WIDTH].astype(_BF16)
    k_ref[...] = proj[:, ATTN_WIDTH:ATTN_WIDTH + KV_WIDTH].astype(_BF16)
    v_ref[...] = proj[:, ATTN_WIDTH + KV_WIDTH:ATTN_WIDTH + 2 * KV_WIDTH].astype(_BF16)
    u_ref[...] = proj[:, ATTN_WIDTH + 2 * KV_WIDTH:]


def _in_proj(x2, g1, w_in):
    n = x2.shape[0]
    row = lambda w: pl.BlockSpec((TM_PROJ, w), lambda i: (i, 0))
    full = lambda a: pl.BlockSpec(a.shape, lambda i: (0,) * a.ndim)
    return pl.pallas_call(
        _in_proj_kernel,
        grid=(n // TM_PROJ,),
        in_specs=[row(D_MODEL), full(g1), full(w_in)],
        out_specs=[row(ATTN_WIDTH), row(KV_WIDTH), row(KV_WIDTH), row(POOL_WIDTH)],
        out_shape=[jax.ShapeDtypeStruct((n, ATTN_WIDTH), _BF16),
                   jax.ShapeDtypeStruct((n, KV_WIDTH), _BF16),
                   jax.ShapeDtypeStruct((n, KV_WIDTH), _BF16),
                   jax.ShapeDtypeStruct((n, POOL_WIDTH), _F32)],
        compiler_params=pltpu.CompilerParams(
            dimension_semantics=("arbitrary",), vmem_limit_bytes=VMEM_LIMIT_DENSE),
        name="in_proj",
    )(x2, g1, w_in)


def _attention_kernel(sink_ref, q_ref, kp_ref, kc_ref, kn_ref, vp_ref, vc_ref, vn_ref, o_ref, *, n_blocks):
    j = pl.program_id(1)
    rows = Q_GROUP * BLOCK
    qpos = lax.broadcasted_iota(jnp.int32, (rows, 3 * BLOCK), 0) % BLOCK
    cpos = lax.broadcasted_iota(jnp.int32, (rows, 3 * BLOCK), 1)
    grp = lax.broadcasted_iota(jnp.int32, (rows, 3 * BLOCK), 0) // BLOCK
    dist = jnp.abs(qpos + BLOCK - cpos)
    lo_c = jnp.where(j == 0, BLOCK, 0)
    hi_c = jnp.where(j == n_blocks - 1, 2 * BLOCK, 3 * BLOCK)
    valid = (dist <= WINDOW) & (cpos >= lo_c) & (cpos < hi_c)
    distf = dist.astype(_F32)
    row_grp = lax.broadcasted_iota(jnp.int32, (rows, 1), 0) // BLOCK
    q = q_ref[0]
    outs = []
    for kh in range(N_KV_HEADS):
        sl = slice(kh * HEAD_DIM, (kh + 1) * HEAD_DIM)
        kb = jnp.concatenate([kp_ref[0][:, sl], kc_ref[0][:, sl], kn_ref[0][:, sl]], axis=0)
        vb = jnp.concatenate([vp_ref[0][:, sl], vc_ref[0][:, sl], vn_ref[0][:, sl]], axis=0)
        qs = jnp.concatenate(
            [q[:, (kh * Q_GROUP + g) * HEAD_DIM:(kh * Q_GROUP + g + 1) * HEAD_DIM] for g in range(Q_GROUP)],
            axis=0)
        logits = lax.dot_general(qs, kb, (((1,), (1,)), ((), ())),
                                 preferred_element_type=_F32) * (HEAD_DIM ** -0.5)
        slope = jnp.zeros((rows, 3 * BLOCK), _F32)
        sink = jnp.zeros((rows, 1), _F32)
        for g in range(Q_GROUP):
            head = kh * Q_GROUP + g
            slope = jnp.where(grp == g, 2.0 ** (-8.0 * (head + 1) / N_Q_HEADS), slope)
            sink = jnp.where(row_grp == g, sink_ref[head], sink)
        logits = jnp.where(valid, logits - slope * distf, NEG_INF)
        m = jnp.maximum(jnp.max(logits, axis=-1, keepdims=True), sink)
        p = jnp.exp(logits - m)
        denom = jnp.sum(p, axis=-1, keepdims=True) + jnp.exp(sink - m)
        o = jnp.dot(p.astype(_BF16), vb, preferred_element_type=_F32) / denom
        outs += [o[g * BLOCK:(g + 1) * BLOCK] for g in range(Q_GROUP)]
    o_ref[0] = jnp.concatenate(outs, axis=-1).astype(_BF16)


def _attention(q3, k3, v3, sink):
    b, s, _ = q3.shape
    nb = s // BLOCK
    cur = lambda w: pl.BlockSpec((1, BLOCK, w), lambda bi, j: (bi, j, 0))
    prv = lambda w: pl.BlockSpec((1, BLOCK, w), lambda bi, j: (bi, jnp.maximum(j - 1, 0), 0))
    nxt = lambda w: pl.BlockSpec((1, BLOCK, w), lambda bi, j: (bi, jnp.minimum(j + 1, nb - 1), 0))
    return pl.pallas_call(
        functools.partial(_attention_kernel, n_blocks=nb),
        grid=(b, nb),
        in_specs=[pl.BlockSpec(memory_space=pltpu.SMEM), cur(ATTN_WIDTH),
                  prv(KV_WIDTH), cur(KV_WIDTH), nxt(KV_WIDTH),
                  prv(KV_WIDTH), cur(KV_WIDTH), nxt(KV_WIDTH)],
        out_specs=cur(ATTN_WIDTH),
        out_shape=jax.ShapeDtypeStruct((b, s, ATTN_WIDTH), _BF16),
        compiler_params=pltpu.CompilerParams(
            dimension_semantics=("arbitrary", "arbitrary"), vmem_limit_bytes=VMEM_LIMIT_DENSE),
        name="attention",
    )(sink, q3, k3, k3, k3, v3, v3, v3)


def _shift_rows(a, k):
    n = a.shape[0]
    return pltpu.roll(a, (-k) % n, axis=0)


def _mix_kernel(x_ref, u_ref, up_ref, un_ref, oa_ref, g1_ref, wg_ref, bg_ref, pw_ref, ps_ref,
                wb_ref, wo_ref, g2_ref, x1_ref, h2_ref, h3_ref, *, seq):
    i = pl.program_id(0)
    t0 = (i * TM_MIX) % seq
    x = x_ref[...]
    h = _rmsnorm(x, g1_ref[...]).astype(_BF16)

    prev = jnp.where(t0 > 0, up_ref[...], 0.0)
    nxt = jnp.where(t0 + TM_MIX < seq, un_ref[...], 0.0)
    ue = jnp.concatenate([prev, u_ref[...], nxt], axis=0)
    t = t0 + lax.broadcasted_iota(jnp.int32, (TM_MIX, 1), 0)
    centre = slice(POOL_HALO, POOL_HALO + TM_MIX)
    pooled = []
    for gi, w in enumerate(POOL_WINDOWS):
        ug = ue[:, gi * POOL_GROUP_DIM:(gi + 1) * POOL_GROUP_DIM]
        acc = ug + _shift_rows(ug, -1)
        half = 1
        while 2 * half < w:
            acc = _shift_rows(acc, half) + _shift_rows(acc, -half)
            half *= 2
        cnt = (jnp.minimum(t + (w - w // 2), seq) - jnp.maximum(t - w // 2, 0)).astype(_F32)
        mixed = acc[centre] / cnt - ug[centre]
        y = jnp.dot(mixed.astype(_BF16), pw_ref[gi], preferred_element_type=_F32)
        pooled.append(y)
    o_pool = jnp.concatenate(pooled, axis=-1) * ps_ref[...]

    gates = jax.nn.sigmoid(jnp.dot(h, wg_ref[...], preferred_element_type=_F32) + bg_ref[...])
    br_a = jnp.dot(oa_ref[...], wb_ref[0], preferred_element_type=_F32)
    br_b = jnp.dot(o_pool.astype(_BF16), wb_ref[1], preferred_element_type=_F32)
    merged = gates[:, :D_MODEL] * br_a + gates[:, D_MODEL:] * br_b
    x1 = x + jnp.dot(merged.astype(_BF16), wo_ref[...], preferred_element_type=_F32)
    x1_ref[...] = x1
    h2 = _rmsnorm(x1, g2_ref[...])
    h2_ref[...] = h2.astype(_BF16)
    for r in range(ROW_VREGS):
        h3_ref[:, r, :] = h2[:, r * LANES:(r + 1) * LANES]


def _mix(x2, u, o_attn, g1, w_gate, b_gate, pool_w, pool_scale, w_branch, w_out, g2, seq):
    n = x2.shape[0]
    per_halo = TM_MIX // POOL_HALO
    n_halo = n // POOL_HALO
    row = lambda w: pl.BlockSpec((TM_MIX, w), lambda i: (i, 0))
    full = lambda a: pl.BlockSpec(a.shape, lambda i: (0,) * a.ndim)
    halo_prev = pl.BlockSpec((POOL_HALO, POOL_WIDTH), lambda i: (jnp.maximum(i * per_halo - 1, 0), 0))
    halo_next = pl.BlockSpec((POOL_HALO, POOL_WIDTH), lambda i: (jnp.minimum((i + 1) * per_halo, n_halo - 1), 0))
    return pl.pallas_call(
        functools.partial(_mix_kernel, seq=seq),
        grid=(n // TM_MIX,),
        in_specs=[row(D_MODEL), row(POOL_WIDTH), halo_prev, halo_next, row(ATTN_WIDTH),
                  full(g1), full(w_gate), full(b_gate), full(pool_w), full(pool_scale),
                  full(w_branch), full(w_out), full(g2)],
        out_specs=[row(D_MODEL), row(D_MODEL),
                   pl.BlockSpec((TM_MIX, ROW_VREGS, LANES), lambda i: (i, 0, 0))],
        out_shape=[jax.ShapeDtypeStruct((n, D_MODEL), _F32), jax.ShapeDtypeStruct((n, D_MODEL), _BF16),
                   jax.ShapeDtypeStruct((n, ROW_VREGS, LANES), _F32)],
        compiler_params=pltpu.CompilerParams(
            dimension_semantics=("arbitrary",), vmem_limit_bytes=VMEM_LIMIT_DENSE),
        name="mix",
    )(x2, u, u, u, o_attn, g1, w_gate, b_gate, pool_w, pool_scale, w_branch, w_out, g2)


def _top_rows(s, order, payload, k):
    vals, picks = [], []
    for _ in range(k):
        m = jnp.max(s, axis=0, keepdims=True)
        am = jnp.min(jnp.where(s == m, order, jnp.inf), axis=0, keepdims=True)
        hit = order == am
        vals.append(m)
        picks.append(am if payload is None else jnp.max(jnp.where(hit, payload, -1.0), axis=0, keepdims=True))
        s = jnp.where(hit, -jnp.inf, s)
    return jnp.concatenate(vals, axis=0), jnp.concatenate(picks, axis=0)


def _staircase(first, second, combine, t):
    half = PEER_TOPK // 2
    rows16 = lax.broadcasted_iota(jnp.int32, (PEER_TOPK, t), 0).astype(_F32)
    rows8 = lax.broadcasted_iota(jnp.int32, (half, t), 0).astype(_F32)
    vals = [combine(first, second[0:1])]
    pos = [rows16 * PEER_TOPK]
    ok = [rows16 >= 0]
    for b in range(1, half):
        vals.append(combine(first[0:half], second[b:b + 1]))
        pos.append(rows8 * PEER_TOPK + b)
        ok.append(rows8 < PEER_TOPK // (b + 1))
    vals.append(combine(first[0:1], second[half:]))
    pos.append(rows8 + half)
    ok.append(rows8 >= 0)
    return jnp.concatenate(vals, axis=0), jnp.concatenate(pos, axis=0), jnp.concatenate(ok, axis=0)


def _route_kernel(h2_ref, wq_ref, keys_ref, idx_ref, gate_ref, qp_ref, e_ref, g_ref):
    t = ROUTE_TOKENS_PER_PICK
    n_tiles = TM_ROUTE // t
    qp_ref[...] = jnp.dot(h2_ref[...], wq_ref[...], preferred_element_type=_F32)
    key_order = lax.broadcasted_iota(jnp.int32, (N_KEYS, t), 0).astype(_F32)

    def head_tile(i, carry):
        hd = i // n_tiles
        tok0 = pl.multiple_of((i % n_tiles) * t, t)
        tops, topi = [], []
        for p in range(2):
            c0 = pl.multiple_of((hd * 2 + p) * PEER_HALF, PEER_HALF)
            qhp = qp_ref[pl.ds(tok0, t), pl.ds(c0, PEER_HALF)].astype(_BF16)
            sc = lax.dot_general(keys_ref[hd, p], qhp, (((1,), (1,)), ((), ())),
                                 preferred_element_type=_F32)
            ts, ti = _top_rows(sc, key_order, None, PEER_TOPK)
            tops.append(ts)
            topi.append(ti)
        cand_s, cand_pos, cand_ok = _staircase(tops[0], tops[1], lambda a, b: a + b, t)
        cand_e, _, _ = _staircase(topi[0], topi[1],
                                  lambda a, b: (a * N_KEYS + b) * TABLE_ROWS_PER_EXPERT, t)
        cand_s = jnp.where(cand_ok, cand_s, -jnp.inf)
        best_s, best_e = _top_rows(cand_s, cand_pos, cand_e, PEER_TOPK)
        ex = jnp.exp(best_s - jnp.max(best_s, axis=0, keepdims=True))
        row0 = pl.multiple_of(hd * PEER_TOPK, PEER_TOPK)
        g_ref[pl.ds(row0, PEER_TOPK), pl.ds(tok0, t)] = ex / jnp.sum(ex, axis=0, keepdims=True)
        e_ref[pl.ds(row0, PEER_TOPK), pl.ds(tok0, t)] = best_e
        return carry

    lax.fori_loop(0, PEER_HEADS * n_tiles, head_tile, 0)
    idx_ref[...] = e_ref[...].T.astype(jnp.int32)
    gate_ref[...] = g_ref[...].T


def _route(h2, w_query, sub_keys):
    n = h2.shape[0]
    full = lambda a: pl.BlockSpec(a.shape, lambda i: (0,) * a.ndim)
    return pl.pallas_call(
        _route_kernel,
        grid=(n // TM_ROUTE,),
        in_specs=[pl.BlockSpec((TM_ROUTE, D_MODEL), lambda i: (i, 0)), full(w_query), full(sub_keys)],
        out_specs=[pl.BlockSpec((TM_ROUTE, N_PAIRS), lambda i: (i, 0)),
                   pl.BlockSpec((TM_ROUTE, N_PAIRS), lambda i: (i, 0))],
        out_shape=[jax.ShapeDtypeStruct((n, N_PAIRS), jnp.int32), jax.ShapeDtypeStruct((n, N_PAIRS), _F32)],
        scratch_shapes=[pltpu.VMEM((TM_ROUTE, 2 * PEER_HEADS * PEER_HALF), _F32),
                        pltpu.VMEM((N_PAIRS, TM_ROUTE), _F32),
                        pltpu.VMEM((N_PAIRS, TM_ROUTE), _F32)],
        compiler_params=pltpu.CompilerParams(
            dimension_semantics=("arbitrary",), vmem_limit_bytes=VMEM_LIMIT_DENSE),
        name="route",
    )(h2, w_query, sub_keys)


def _expert_slab(tab_ref, row_ref, k):
    r = pl.multiple_of(row_ref[k], TABLE_ROWS_PER_EXPERT)
    return pltpu.bitcast(tab_ref[pl.ds(r, TABLE_ROWS_PER_EXPERT), :], _BF16).astype(_F32)


_FOLD_ROW_OF_SLOT = (0, 4, 2, 6, 1, 5, 3, 7)


def _fold8(slabs, sub):
    def halve(pairs, step, keep):
        out = []
        for a, b in pairs:
            if step == SUBLANES // 2:
                out.append(jnp.where(keep, a, b) + pltpu.roll(jnp.where(keep, b, a), step, 0))
            else:
                out.append(jnp.where(keep, a, pltpu.roll(b, step, 0))
                           + jnp.where(keep, pltpu.roll(a, SUBLANES - step, 0), b))
        return out

    c = halve([(slabs[2 * i], slabs[2 * i + 1]) for i in range(4)], 4, sub < 4)
    e = halve([(c[0], c[1]), (c[2], c[3])], 2, (sub % 4) < 2)
    return halve([(e[0], e[1])], 1, (sub % 2) == 0)[0]


def _peer_up_kernel(idx_ref, h3_ref, gate_ref, tab_ref, act_ref, pend_ref, s_ref):
    sub = lax.broadcasted_iota(jnp.int32, (SUBLANES, LANES), 0)

    def fold_token(t):
        h = h3_ref[t]
        row_ref = idx_ref.at[t]
        folded = []
        for grp in range(N_PAIRS // SUBLANES):
            prods = []
            for slot in range(SUBLANES):
                k = grp * SUBLANES + _FOLD_ROW_OF_SLOT[slot]
                prods.append(_expert_slab(tab_ref, row_ref, k) * h)
            folded.append(_fold8(prods, sub))
        return jnp.concatenate(folded, axis=0)

    def finish(slot, t):
        s_ref[pl.ds(t, 1), :] = jnp.sum(pend_ref[slot].T, axis=0, keepdims=True)

    pend_ref[...] = jnp.zeros_like(pend_ref)

    def pair(j, carry):
        finish(0, jnp.maximum(2 * j - 2, 0))
        finish(1, jnp.maximum(2 * j - 1, 0))
        pend_ref[0] = fold_token(2 * j)
        pend_ref[1] = fold_token(2 * j + 1)
        return carry

    lax.fori_loop(0, TB_PEER // 2, pair, 0)
    finish(0, TB_PEER - 2)
    finish(1, TB_PEER - 1)
    s = s_ref[...]
    act_ref[...] = 0.5 * s * (1.0 + lax.erf(s * (2.0 ** -0.5))) * gate_ref[...]


def _gather_to_tile(idx_ref, t, tab_ref, tile_ref):
    row_ref = idx_ref.at[t]
    for k in range(N_PAIRS):
        tile_ref[pl.ds(k, ROW_VREGS, stride=TILE_STRIDE), :] = _expert_slab(tab_ref, row_ref, k)


def _tile_rows(tile_ref):
    return jnp.concatenate(
        [tile_ref[pl.ds(r * TILE_STRIDE, N_PAIRS), :] for r in range(ROW_VREGS)], axis=-1).astype(_BF16)


def _split_bf16(a):
    hi = a.astype(_BF16)
    lo = (a - hi.astype(_F32)).astype(_BF16)
    return jnp.concatenate([hi, lo], axis=0)


LOOKAHEAD = 2


def _idx_chunk_copy(idx_hbm, first_token, buf, sem):
    first_row = pl.program_id(0) * TB_PEER + first_token
    return pltpu.make_async_copy(idx_hbm.at[pl.ds(first_row, IDX_CHUNK)], buf, sem)


def _peer_down_kernel(idx_hbm, act_ref, x1_ref, tab_ref, g_ref, o_ref,
                      tile_a, tile_b, y_ref, idx_a, idx_b, sem):
    tiles = (tile_a, tile_b)

    def consume(t, tile_ref):
        lhs = _split_bf16(act_ref[pl.ds(t, 1), :])
        y2 = jnp.dot(lhs, _tile_rows(tile_ref), preferred_element_type=_F32)
        y_ref[pl.ds(t, 1), :] = y2[0:1] + y2[1:2]

    @pl.when(pl.program_id(0) == 0)
    def _():
        head = _idx_chunk_copy(idx_hbm, 0, idx_b, sem.at[1])
        ahead = _idx_chunk_copy(idx_hbm, LOOKAHEAD, idx_a, sem.at[0])
        head.start()
        ahead.start()
        head.wait()
        for lt in range(LOOKAHEAD):
            _gather_to_tile(idx_b, lt, tab_ref, tiles[lt % 2])
        ahead.wait()

    def two_chunks(j, carry):
        t0 = 2 * IDX_CHUNK * j
        fill_b = _idx_chunk_copy(idx_hbm, t0 + IDX_CHUNK + LOOKAHEAD, idx_b, sem.at[1])
        fill_b.start()
        for lt in range(IDX_CHUNK):
            consume(t0 + lt, tiles[lt % 2])
            _gather_to_tile(idx_a, lt, tab_ref, tiles[lt % 2])
        fill_b.wait()
        fill_a = _idx_chunk_copy(idx_hbm, t0 + 2 * IDX_CHUNK + LOOKAHEAD, idx_a, sem.at[0])
        fill_a.start()
        for lt in range(IDX_CHUNK):
            consume(t0 + IDX_CHUNK + lt, tiles[lt % 2])
            _gather_to_tile(idx_b, lt, tab_ref, tiles[lt % 2])
        fill_a.wait()
        return carry

    lax.fori_loop(0, TB_PEER // (2 * IDX_CHUNK), two_chunks, 0)
    o_ref[...] = _rmsnorm(x1_ref[...] + y_ref[...], g_ref[...])


def _peer_specs():
    tok = lambda w: pl.BlockSpec((TB_PEER, w), lambda i: (i, 0))
    idx = pl.BlockSpec((TB_PEER, N_PAIRS), lambda i: (i, 0), memory_space=pltpu.SMEM)
    table = pl.BlockSpec(memory_space=pltpu.VMEM)
    params = pltpu.CompilerParams(dimension_semantics=("arbitrary",), vmem_limit_bytes=VMEM_LIMIT_PEER)
    return tok, idx, table, params


def _peer_up(idx, h3, gate, table):
    n = h3.shape[0]
    tok, idx_spec, table_spec, params = _peer_specs()
    return pl.pallas_call(
        _peer_up_kernel,
        grid=(n // TB_PEER,),
        in_specs=[idx_spec, pl.BlockSpec((TB_PEER, ROW_VREGS, LANES), lambda i: (i, 0, 0)),
                  tok(N_PAIRS), table_spec],
        out_specs=tok(N_PAIRS),
        out_shape=jax.ShapeDtypeStruct((n, N_PAIRS), _F32),
        scratch_shapes=[pltpu.VMEM((2, N_PAIRS, LANES), _F32), pltpu.VMEM((TB_PEER, N_PAIRS), _F32)],
        compiler_params=params,
        name="peer_up",
    )(idx, h3, gate, table)


def _peer_down(idx, act, x1, table, g_final):
    n = x1.shape[0]
    idx = jnp.pad(idx, ((0, 2 * IDX_CHUNK), (0, 0)))
    tok, idx_spec, table_spec, params = _peer_specs()
    tile = pltpu.VMEM((ROW_VREGS * TILE_STRIDE, LANES), _F32)
    return pl.pallas_call(
        _peer_down_kernel,
        grid=(n // TB_PEER,),
        in_specs=[pl.BlockSpec(memory_space=pl.ANY), tok(N_PAIRS), tok(D_MODEL), table_spec,
                  pl.BlockSpec(g_final.shape, lambda i: (0, 0))],
        out_specs=tok(D_MODEL),
        out_shape=jax.ShapeDtypeStruct((n, D_MODEL), _F32),
        scratch_shapes=[tile, tile, pltpu.VMEM((TB_PEER, D_MODEL), _F32),
                        pltpu.SMEM((IDX_CHUNK, N_PAIRS), jnp.int32),
                        pltpu.SMEM((IDX_CHUNK, N_PAIRS), jnp.int32),
                        pltpu.SemaphoreType.DMA((2,))],
        compiler_params=params,
        name="peer_down",
    )(idx, act, x1, table, g_final)


def _pack_table_kernel(t_ref, o_ref):
    x = t_ref[...]
    for i in range(TABLE_ROWS_PER_EXPERT):
        lo = x[:, (2 * i) * LANES:(2 * i + 1) * LANES]
        hi = x[:, (2 * i + 1) * LANES:(2 * i + 2) * LANES]
        words = pltpu.pack_elementwise([lo, hi], packed_dtype=_BF16)
        o_ref[pl.ds(i, TE_PACK, stride=TABLE_ROWS_PER_EXPERT), :] = pltpu.bitcast(words, jnp.uint32)


def _expert_table(t):
    return pl.pallas_call(
        _pack_table_kernel,
        grid=(N_EXPERTS // TE_PACK,),
        in_specs=[pl.BlockSpec((TE_PACK, D_MODEL), lambda i: (i, 0))],
        out_specs=pl.BlockSpec((TE_PACK * TABLE_ROWS_PER_EXPERT, LANES), lambda i: (i, 0)),
        out_shape=jax.ShapeDtypeStruct((N_EXPERTS * TABLE_ROWS_PER_EXPERT, LANES), jnp.uint32),
        compiler_params=pltpu.CompilerParams(
            dimension_semantics=("arbitrary",), vmem_limit_bytes=VMEM_LIMIT_DENSE),
        name="pack_table",
    )(t)


def kernel(x, norm1_g, w_in, sink_logits, pool_w, pool_scale, w_branch, w_gate, b_gate, w_out,
           norm2_g, peer_w_query, peer_sub_keys, peer_u, peer_v, norm_final_g):
    b, s, d = x.shape
    n = b * s
    assert norm1_g.shape[0] == 1 and d == D_MODEL
    assert s % TM_MIX == 0 and s % BLOCK == 0 and n % TM_PROJ == 0 and n % TM_ROUTE == 0 and n % TB_PEER == 0
    row = lambda a: a.reshape(1, -1)
    x2 = x.reshape(n, d)
    g1 = row(norm1_g[0])
    q, k, v, u = _in_proj(x2, g1, w_in[0].astype(_BF16))
    o_attn = _attention(q.reshape(b, s, ATTN_WIDTH), k.reshape(b, s, KV_WIDTH),
                        v.reshape(b, s, KV_WIDTH), sink_logits[0]).reshape(n, ATTN_WIDTH)
    x1, h2, h3 = _mix(x2, u, o_attn, g1, w_gate[0].astype(_BF16), row(b_gate[0]),
                  pool_w[0].astype(_BF16), row(pool_scale[0]), w_branch[0].astype(_BF16),
                  w_out[0].astype(_BF16), row(norm2_g[0]), s)
    idx, gate = _route(h2, peer_w_query[0].astype(_BF16), peer_sub_keys[0].astype(_BF16))
    act = _peer_up(idx, h3, gate, _expert_table(peer_u[0]))
    y = _peer_down(idx, act, x1, _expert_table(peer_v[0]), row(norm_final_g))
    return y.reshape(b, s, d)
```

```python
import functools

import jax
import jax.numpy as jnp
from jax import lax
from jax.experimental import pallas as pl
from jax.experimental.pallas import tpu as pltpu

D_MODEL = 1024
N_Q_HEADS = 8
N_KV_HEADS = 2
HEAD_DIM = 64
Q_GROUP = N_Q_HEADS // N_KV_HEADS
ATTN_WIDTH = N_Q_HEADS * HEAD_DIM
KV_WIDTH = N_KV_HEADS * HEAD_DIM
WINDOW = 128
BLOCK = 128
POOL_WINDOWS = (2, 4, 8, 16)
POOL_WIDTH = D_MODEL // 2
POOL_GROUP_DIM = POOL_WIDTH // len(POOL_WINDOWS)
IN_WIDTH = ATTN_WIDTH + 2 * KV_WIDTH + POOL_WIDTH
PEER_HEADS = 8
N_KEYS = 128
N_EXPERTS = N_KEYS * N_KEYS
PEER_HALF = 128
PEER_TOPK = 16
N_PAIRS = PEER_HEADS * PEER_TOPK
EPS = 1e-6
NEG_INF = -1e30

SUBLANES = 8
LANES = 128
POOL_HALO = 8
ROW_VREGS = D_MODEL // LANES
TABLE_ROWS_PER_EXPERT = ROW_VREGS // 2
TILE_STRIDE = N_PAIRS + 1

TM_PROJ = 512
TM_MIX = 256
TM_ROUTE = 512
ROUTE_TOKENS_PER_PICK = 512
TB_PEER = 128
TE_PACK = 512
IDX_CHUNK = 16
VMEM_LIMIT_DENSE = 48 * 1024 * 1024
VMEM_LIMIT_PEER = 56 * 1024 * 1024

_F32 = jnp.float32
_BF16 = jnp.bfloat16


def _rmsnorm(x, g):
    return x * lax.rsqrt(jnp.mean(x * x, axis=-1, keepdims=True) + EPS) * g


def _in_proj_kernel(x_ref, g_ref, w_ref, q_ref, k_ref, v_ref, u_ref):
    h = _rmsnorm(x_ref[...], g_ref[...])
    proj = jnp.dot(h.astype(_BF16), w_ref[...], preferred_element_type=_F32)
    q_ref[...] = proj[:, :ATTN_WIDTH].astype(_BF16)
    k_ref[...] = proj[:, ATTN_WIDTH:ATTN_WIDTH + KV_WIDTH].astype(_BF16)
    v_ref[...] = proj[:, ATTN_WIDTH + KV_WIDTH:ATTN_WIDTH + 2 * KV_WIDTH].astype(_BF16)
    u_ref[...] = proj[:, ATTN_WIDTH + 2 * KV_WIDTH:]


def _in_proj(x2, g1, w_in):
    n = x2.shape[0]
    row = lambda w: pl.BlockSpec((TM_PROJ, w), lambda i: (i, 0))
    full = lambda a: pl.BlockSpec(a.shape, lambda i: (0,) * a.ndim)
    return pl.pallas_call(
        _in_proj_kernel,
        grid=(n // TM_PROJ,),
        in_specs=[row(D_MODEL), full(g1), full(w_in)],
        out_specs=[row(ATTN_WIDTH), row(KV_WIDTH), row(KV_WIDTH), row(POOL_WIDTH)],
        out_shape=[jax.ShapeDtypeStruct((n, ATTN_WIDTH), _BF16),
                   jax.ShapeDtypeStruct((n, KV_WIDTH), _BF16),
                   jax.ShapeDtypeStruct((n, KV_WIDTH), _BF16),
                   jax.ShapeDtypeStruct((n, POOL_WIDTH), _F32)],
        compiler_params=pltpu.CompilerParams(
            dimension_semantics=("arbitrary",), vmem_limit_bytes=VMEM_LIMIT_DENSE),
        name="in_proj",
    )(x2, g1, w_in)


def _attention_kernel(sink_ref, q_ref, kp_ref, kc_ref, kn_ref, vp_ref, vc_ref, vn_ref, o_ref, *, n_blocks):
    j = pl.program_id(1)
    rows = Q_GROUP * BLOCK
    qpos = lax.broadcasted_iota(jnp.int32, (rows, 3 * BLOCK), 0) % BLOCK
    cpos = lax.broadcasted_iota(jnp.int32, (rows, 3 * BLOCK), 1)
    grp = lax.broadcasted_iota(jnp.int32, (rows, 3 * BLOCK), 0) // BLOCK
    dist = jnp.abs(qpos + BLOCK - cpos)
    lo_c = jnp.where(j == 0, BLOCK, 0)
    hi_c = jnp.where(j == n_blocks - 1, 2 * BLOCK, 3 * BLOCK)
    valid = (dist <= WINDOW) & (cpos >= lo_c) & (cpos < hi_c)
    distf = dist.astype(_F32)
    row_grp = lax.broadcasted_iota(jnp.int32, (rows, 1), 0) // BLOCK
    q = q_ref[0]
    outs = []
    for kh in range(N_KV_HEADS):
        sl = slice(kh * HEAD_DIM, (kh + 1) * HEAD_DIM)
        kb = jnp.concatenate([kp_ref[0][:, sl], kc_ref[0][:, sl], kn_ref[0][:, sl]], axis=0)
        vb = jnp.concatenate([vp_ref[0][:, sl], vc_ref[0][:, sl], vn_ref[0][:, sl]], axis=0)
        qs = jnp.concatenate(
            [q[:, (kh * Q_GROUP + g) * HEAD_DIM:(kh * Q_GROUP + g + 1) * HEAD_DIM] for g in range(Q_GROUP)],
            axis=0)
        logits = lax.dot_general(qs, kb, (((1,), (1,)), ((), ())),
                                 preferred_element_type=_F32) * (HEAD_DIM ** -0.5)
        slope = jnp.zeros((rows, 3 * BLOCK), _F32)
        sink = jnp.zeros((rows, 1), _F32)
        for g in range(Q_GROUP):
            head = kh * Q_GROUP + g
            slope = jnp.where(grp == g, 2.0 ** (-8.0 * (head + 1) / N_Q_HEADS), slope)
            sink = jnp.where(row_grp == g, sink_ref[head], sink)
        logits = jnp.where(valid, logits - slope * distf, NEG_INF)
        m = jnp.maximum(jnp.max(logits, axis=-1, keepdims=True), sink)
        p = jnp.exp(logits - m)
        denom = jnp.sum(p, axis=-1, keepdims=True) + jnp.exp(sink - m)
        o = jnp.dot(p.astype(_BF16), vb, preferred_element_type=_F32) / denom
        outs += [o[g * BLOCK:(g + 1) * BLOCK] for g in range(Q_GROUP)]
    o_ref[0] = jnp.concatenate(outs, axis=-1).astype(_BF16)


def _attention(q3, k3, v3, sink):
    b, s, _ = q3.shape
    nb = s // BLOCK
    cur = lambda w: pl.BlockSpec((1, BLOCK, w), lambda bi, j: (bi, j, 0))
    prv = lambda w: pl.BlockSpec((1, BLOCK, w), lambda bi, j: (bi, jnp.maximum(j - 1, 0), 0))
    nxt = lambda w: pl.BlockSpec((1, BLOCK, w), lambda bi, j: (bi, jnp.minimum(j + 1, nb - 1), 0))
    return pl.pallas_call(
        functools.partial(_attention_kernel, n_blocks=nb),
        grid=(b, nb),
        in_specs=[pl.BlockSpec(memory_space=pltpu.SMEM), cur(ATTN_WIDTH),
                  prv(KV_WIDTH), cur(KV_WIDTH), nxt(KV_WIDTH),
                  prv(KV_WIDTH), cur(KV_WIDTH), nxt(KV_WIDTH)],
        out_specs=cur(ATTN_WIDTH),
        out_shape=jax.ShapeDtypeStruct((b, s, ATTN_WIDTH), _BF16),
        compiler_params=pltpu.CompilerParams(
            dimension_semantics=("arbitrary", "arbitrary"), vmem_limit_bytes=VMEM_LIMIT_DENSE),
        name="attention",
    )(sink, q3, k3, k3, k3, v3, v3, v3)


def _shift_rows(a, k):
    n = a.shape[0]
    return pltpu.roll(a, (-k) % n, axis=0)


def _mix_kernel(x_ref, u_ref, up_ref, un_ref, oa_ref, g1_ref, wg_ref, bg_ref, pw_ref, ps_ref,
                wb_ref, wo_ref, g2_ref, x1_ref, h2_ref, h3_ref, *, seq):
    i = pl.program_id(0)
    t0 = (i * TM_MIX) % seq
    x = x_ref[...]
    h = _rmsnorm(x, g1_ref[...]).astype(_BF16)

    prev = jnp.where(t0 > 0, up_ref[...], 0.0)
    nxt = jnp.where(t0 + TM_MIX < seq, un_ref[...], 0.0)
    ue = jnp.concatenate([prev, u_ref[...], nxt], axis=0)
    t = t0 + lax.broadcasted_iota(jnp.int32, (TM_MIX, 1), 0)
    centre = slice(POOL_HALO, POOL_HALO + TM_MIX)
    pooled = []
    for gi, w in enumerate(POOL_WINDOWS):
        ug = ue[:, gi * POOL_GROUP_DIM:(gi + 1) * POOL_GROUP_DIM]
        acc = ug + _shift_rows(ug, -1)
        half = 1
        while 2 * half < w:
            acc = _shift_rows(acc, half) + _shift_rows(acc, -half)
            half *= 2
        cnt = (jnp.minimum(t + (w - w // 2), seq) - jnp.maximum(t - w // 2, 0)).astype(_F32)
        mixed = acc[centre] / cnt - ug[centre]
        y = jnp.dot(mixed.astype(_BF16), pw_ref[gi], preferred_element_type=_F32)
        pooled.append(y)
    o_pool = jnp.concatenate(pooled, axis=-1) * ps_ref[...]

    gates = jax.nn.sigmoid(jnp.dot(h, wg_ref[...], preferred_element_type=_F32) + bg_ref[...])
    br_a = jnp.dot(oa_ref[...], wb_ref[0], preferred_element_type=_F32)
    br_b = jnp.dot(o_pool.astype(_BF16), wb_ref[1], preferred_element_type=_F32)
    merged = gates[:, :D_MODEL] * br_a + gates[:, D_MODEL:] * br_b
    x1 = x + jnp.dot(merged.astype(_BF16), wo_ref[...], preferred_element_type=_F32)
    x1_ref[...] = x1
    h2 = _rmsnorm(x1, g2_ref[...])
    h2_ref[...] = h2.astype(_BF16)
    for r in range(ROW_VREGS):
        h3_ref[:, r, :] = h2[:, r * LANES:(r + 1) * LANES]


def _mix(x2, u, o_attn, g1, w_gate, b_gate, pool_w, pool_scale, w_branch, w_out, g2, seq):
    n = x2.shape[0]
    per_halo = TM_MIX // POOL_HALO
    n_halo = n // POOL_HALO
    row = lambda w: pl.BlockSpec((TM_MIX, w), lambda i: (i, 0))
    full = lambda a: pl.BlockSpec(a.shape, lambda i: (0,) * a.ndim)
    halo_prev = pl.BlockSpec((POOL_HALO, POOL_WIDTH), lambda i: (jnp.maximum(i * per_halo - 1, 0), 0))
    halo_next = pl.BlockSpec((POOL_HALO, POOL_WIDTH), lambda i: (jnp.minimum((i + 1) * per_halo, n_halo - 1), 0))
    return pl.pallas_call(
        functools.partial(_mix_kernel, seq=seq),
        grid=(n // TM_MIX,),
        in_specs=[row(D_MODEL), row(POOL_WIDTH), halo_prev, halo_next, row(ATTN_WIDTH),
                  full(g1), full(w_gate), full(b_gate), full(pool_w), full(pool_scale),
                  full(w_branch), full(w_out), full(g2)],
        out_specs=[row(D_MODEL), row(D_MODEL),
                   pl.BlockSpec((TM_MIX, ROW_VREGS, LANES), lambda i: (i, 0, 0))],
        out_shape=[jax.ShapeDtypeStruct((n, D_MODEL), _F32), jax.ShapeDtypeStruct((n, D_MODEL), _BF16),
                   jax.ShapeDtypeStruct((n, ROW_VREGS, LANES), _F32)],
        compiler_params=pltpu.CompilerParams(
            dimension_semantics=("arbitrary",), vmem_limit_bytes=VMEM_LIMIT_DENSE),
        name="mix",
    )(x2, u, u, u, o_attn, g1, w_gate, b_gate, pool_w, pool_scale, w_branch, w_out, g2)


def _top_rows(s, order, payload, k):
    vals, picks = [], []
    for _ in range(k):
        m = jnp.max(s, axis=0, keepdims=True)
        am = jnp.min(jnp.where(s == m, order, jnp.inf), axis=0, keepdims=True)
        hit = order == am
        vals.append(m)
        picks.append(am if payload is None else jnp.max(jnp.where(hit, payload, -1.0), axis=0, keepdims=True))
        s = jnp.where(hit, -jnp.inf, s)
    return jnp.concatenate(vals, axis=0), jnp.concatenate(picks, axis=0)


def _staircase(first, second, combine, t):
    half = PEER_TOPK // 2
    rows16 = lax.broadcasted_iota(jnp.int32, (PEER_TOPK, t), 0).astype(_F32)
    rows8 = lax.broadcasted_iota(jnp.int32, (half, t), 0).astype(_F32)
    vals = [combine(first, second[0:1])]
    pos = [rows16 * PEER_TOPK]
    ok = [rows16 >= 0]
    for b in range(1, half):
        vals.append(combine(first[0:half], second[b:b + 1]))
        pos.append(rows8 * PEER_TOPK + b)
        ok.append(rows8 < PEER_TOPK // (b + 1))
    vals.append(combine(first[0:1], second[half:]))
    pos.append(rows8 + half)
    ok.append(rows8 >= 0)
    return jnp.concatenate(vals, axis=0), jnp.concatenate(pos, axis=0), jnp.concatenate(ok, axis=0)


def _route_kernel(h2_ref, wq_ref, keys_ref, idx_ref, gate_ref, qp_ref, e_ref, g_ref):
    t = ROUTE_TOKENS_PER_PICK
    n_tiles = TM_ROUTE // t
    qp_ref[...] = jnp.dot(h2_ref[...], wq_ref[...], preferred_element_type=_F32)
    key_order = lax.broadcasted_iota(jnp.int32, (N_KEYS, t), 0).astype(_F32)

    def head_tile(i, carry):
        hd = i // n_tiles
        tok0 = pl.multiple_of((i % n_tiles) * t, t)
        tops, topi = [], []
        for p in range(2):
            c0 = pl.multiple_of((hd * 2 + p) * PEER_HALF, PEER_HALF)
            qhp = qp_ref[pl.ds(tok0, t), pl.ds(c0, PEER_HALF)].astype(_BF16)
            sc = lax.dot_general(keys_ref[hd, p], qhp, (((1,), (1,)), ((), ())),
                                 preferred_element_type=_F32)
            ts, ti = _top_rows(sc, key_order, None, PEER_TOPK)
            tops.append(ts)
            topi.append(ti)
        cand_s, cand_pos, cand_ok = _staircase(tops[0], tops[1], lambda a, b: a + b, t)
        cand_e, _, _ = _staircase(topi[0], topi[1],
                                  lambda a, b: (a * N_KEYS + b) * TABLE_ROWS_PER_EXPERT, t)
        cand_s = jnp.where(cand_ok, cand_s, -jnp.inf)
        best_s, best_e = _top_rows(cand_s, cand_pos, cand_e, PEER_TOPK)
        ex = jnp.exp(best_s - jnp.max(best_s, axis=0, keepdims=True))
        row0 = pl.multiple_of(hd * PEER_TOPK, PEER_TOPK)
        g_ref[pl.ds(row0, PEER_TOPK), pl.ds(tok0, t)] = ex / jnp.sum(ex, axis=0, keepdims=True)
        e_ref[pl.ds(row0, PEER_TOPK), pl.ds(tok0, t)] = best_e
        return carry

    lax.fori_loop(0, PEER_HEADS * n_tiles, head_tile, 0)
    idx_ref[...] = e_ref[...].T.astype(jnp.int32)
    gate_ref[...] = g_ref[...].T


def _route(h2, w_query, sub_keys):
    n = h2.shape[0]
    full = lambda a: pl.BlockSpec(a.shape, lambda i: (0,) * a.ndim)
    return pl.pallas_call(
        _route_kernel,
        grid=(n // TM_ROUTE,),
        in_specs=[pl.BlockSpec((TM_ROUTE, D_MODEL), lambda i: (i, 0)), full(w_query), full(sub_keys)],
        out_specs=[pl.BlockSpec((TM_ROUTE, N_PAIRS), lambda i: (i, 0)),
                   pl.BlockSpec((TM_ROUTE, N_PAIRS), lambda i: (i, 0))],
        out_shape=[jax.ShapeDtypeStruct((n, N_PAIRS), jnp.int32), jax.ShapeDtypeStruct((n, N_PAIRS), _F32)],
        scratch_shapes=[pltpu.VMEM((TM_ROUTE, 2 * PEER_HEADS * PEER_HALF), _F32),
                        pltpu.VMEM((N_PAIRS, TM_ROUTE), _F32),
                        pltpu.VMEM((N_PAIRS, TM_ROUTE), _F32)],
        compiler_params=pltpu.CompilerParams(
            dimension_semantics=("arbitrary",), vmem_limit_bytes=VMEM_LIMIT_DENSE),
        name="route",
    )(h2, w_query, sub_keys)


def _expert_slab(tab_ref, row_ref, k):
    r = pl.multiple_of(row_ref[k], TABLE_ROWS_PER_EXPERT)
    return pltpu.bitcast(tab_ref[pl.ds(r, TABLE_ROWS_PER_EXPERT), :], _BF16).astype(_F32)


_FOLD_ROW_OF_SLOT = (0, 4, 2, 6, 1, 5, 3, 7)


def _fold8(slabs, sub):
    def halve(pairs, step, keep):
        out = []
        for a, b in pairs:
            if step == SUBLANES // 2:
                out.append(jnp.where(keep, a, b) + pltpu.roll(jnp.where(keep, b, a), step, 0))
            else:
                out.append(jnp.where(keep, a, pltpu.roll(b, step, 0))
                           + jnp.where(keep, pltpu.roll(a, SUBLANES - step, 0), b))
        return out

    c = halve([(slabs[2 * i], slabs[2 * i + 1]) for i in range(4)], 4, sub < 4)
    e = halve([(c[0], c[1]), (c[2], c[3])], 2, (sub % 4) < 2)
    return halve([(e[0], e[1])], 1, (sub % 2) == 0)[0]


def _peer_up_kernel(idx_ref, h3_ref, gate_ref, tab_ref, act_ref, pend_ref, s_ref):
    sub = lax.broadcasted_iota(jnp.int32, (SUBLANES, LANES), 0)

    def fold_token(t):
        h = h3_ref[t]
        row_ref = idx_ref.at[t]
        folded = []
        for grp in range(N_PAIRS // SUBLANES):
            prods = []
            for slot in range(SUBLANES):
                k = grp * SUBLANES + _FOLD_ROW_OF_SLOT[slot]
                prods.append(_expert_slab(tab_ref, row_ref, k) * h)
            folded.append(_fold8(prods, sub))
        return jnp.concatenate(folded, axis=0)

    def finish(slot, t):
        s_ref[pl.ds(t, 1), :] = jnp.sum(pend_ref[slot].T, axis=0, keepdims=True)

    pend_ref[...] = jnp.zeros_like(pend_ref)

    def pair(j, carry):
        finish(0, jnp.maximum(2 * j - 2, 0))
        finish(1, jnp.maximum(2 * j - 1, 0))
        pend_ref[0] = fold_token(2 * j)
        pend_ref[1] = fold_token(2 * j + 1)
        return carry

    lax.fori_loop(0, TB_PEER // 2, pair, 0)
    finish(0, TB_PEER - 2)
    finish(1, TB_PEER - 1)
    s = s_ref[...]
    act_ref[...] = 0.5 * s * (1.0 + lax.erf(s * (2.0 ** -0.5))) * gate_ref[...]


def _gather_to_tile(idx_ref, t, tab_ref, tile_ref):
    row_ref = idx_ref.at[t]
    for k in range(N_PAIRS):
        tile_ref[pl.ds(k, ROW_VREGS, stride=TILE_STRIDE), :] = _expert_slab(tab_ref, row_ref, k)


def _tile_rows(tile_ref):
    return jnp.concatenate(
        [tile_ref[pl.ds(r * TILE_STRIDE, N_PAIRS), :] for r in range(ROW_VREGS)], axis=-1).astype(_BF16)


def _split_bf16(a):
    hi = a.astype(_BF16)
    lo = (a - hi.astype(_F32)).astype(_BF16)
    return jnp.concatenate([hi, lo], axis=0)


def _peer_down_kernel(idx_vmem, act_ref, x1_ref, tab_ref, g_ref, o_ref,
                      tile_a, tile_b, y_ref, idx_a, idx_b, sem):
    tiles = (tile_a, tile_b)
    n_chunks = TB_PEER // IDX_CHUNK

    def chunk_copy(chunk, buf, slot):
        start = pl.multiple_of(jnp.minimum(chunk, n_chunks - 1) * IDX_CHUNK, IDX_CHUNK)
        return pltpu.make_async_copy(idx_vmem.at[pl.ds(start, IDX_CHUNK)], buf, sem.at[slot])

    def consume(t, tile_ref):
        lhs = _split_bf16(act_ref[pl.ds(t, 1), :])
        y2 = jnp.dot(lhs, _tile_rows(tile_ref), preferred_element_type=_F32)
        y_ref[pl.ds(t, 1), :] = y2[0:1] + y2[1:2]

    def step(t, src_buf, src_row):
        consume(t, tiles[src_row % 2])
        _gather_to_tile(src_buf, src_row, tab_ref, tiles[src_row % 2])

    first = chunk_copy(0, idx_a, 0)
    first.start()
    first.wait()
    _gather_to_tile(idx_a, 0, tab_ref, tile_a)
    _gather_to_tile(idx_a, 1, tab_ref, tile_b)

    def two_chunks(j, carry):
        t0 = 2 * IDX_CHUNK * j
        fill_b = chunk_copy(2 * j + 1, idx_b, 1)
        fill_b.start()
        for lt in range(IDX_CHUNK - 2):
            step(t0 + lt, idx_a, lt + 2)
        fill_b.wait()
        for lt in range(IDX_CHUNK - 2, IDX_CHUNK):
            step(t0 + lt, idx_b, lt + 2 - IDX_CHUNK)
        fill_a = chunk_copy(2 * j + 2, idx_a, 0)
        fill_a.start()
        for lt in range(IDX_CHUNK, 2 * IDX_CHUNK - 2):
            step(t0 + lt, idx_b, lt + 2 - IDX_CHUNK)
        fill_a.wait()
        for lt in range(2 * IDX_CHUNK - 2, 2 * IDX_CHUNK):
            step(t0 + lt, idx_a, lt + 2 - 2 * IDX_CHUNK)
        return carry

    lax.fori_loop(0, TB_PEER // (2 * IDX_CHUNK), two_chunks, 0)
    o_ref[...] = _rmsnorm(x1_ref[...] + y_ref[...], g_ref[...])


def _peer_specs():
    tok = lambda w: pl.BlockSpec((TB_PEER, w), lambda i: (i, 0))
    idx = pl.BlockSpec((TB_PEER, N_PAIRS), lambda i: (i, 0), memory_space=pltpu.SMEM)
    table = pl.BlockSpec(memory_space=pltpu.VMEM)
    params = pltpu.CompilerParams(dimension_semantics=("arbitrary",), vmem_limit_bytes=VMEM_LIMIT_PEER)
    return tok, idx, table, params


def _peer_up(idx, h3, gate, table):
    n = h3.shape[0]
    tok, idx_spec, table_spec, params = _peer_specs()
    return pl.pallas_call(
        _peer_up_kernel,
        grid=(n // TB_PEER,),
        in_specs=[idx_spec, pl.BlockSpec((TB_PEER, ROW_VREGS, LANES), lambda i: (i, 0, 0)),
                  tok(N_PAIRS), table_spec],
        out_specs=tok(N_PAIRS),
        out_shape=jax.ShapeDtypeStruct((n, N_PAIRS), _F32),
        scratch_shapes=[pltpu.VMEM((2, N_PAIRS, LANES), _F32), pltpu.VMEM((TB_PEER, N_PAIRS), _F32)],
        compiler_params=params,
        name="peer_up",
    )(idx, h3, gate, table)


def _peer_down(idx, act, x1, table, g_final):
    n = x1.shape[0]
    tok, idx_spec, table_spec, params = _peer_specs()
    tile = pltpu.VMEM((ROW_VREGS * TILE_STRIDE, LANES), _F32)
    return pl.pallas_call(
        _peer_down_kernel,
        grid=(n // TB_PEER,),
        in_specs=[tok(N_PAIRS), tok(N_PAIRS), tok(D_MODEL), table_spec,
                  pl.BlockSpec(g_final.shape, lambda i: (0, 0))],
        out_specs=tok(D_MODEL),
        out_shape=jax.ShapeDtypeStruct((n, D_MODEL), _F32),
        scratch_shapes=[tile, tile, pltpu.VMEM((TB_PEER, D_MODEL), _F32),
                        pltpu.SMEM((IDX_CHUNK, N_PAIRS), jnp.int32),
                        pltpu.SMEM((IDX_CHUNK, N_PAIRS), jnp.int32),
                        pltpu.SemaphoreType.DMA((2,))],
        compiler_params=params,
        name="peer_down",
    )(idx, act, x1, table, g_final)


def _pack_table_kernel(t_ref, o_ref):
    x = t_ref[...]
    for i in range(TABLE_ROWS_PER_EXPERT):
        lo = x[:, (2 * i) * LANES:(2 * i + 1) * LANES]
        hi = x[:, (2 * i + 1) * LANES:(2 * i + 2) * LANES]
        words = pltpu.pack_elementwise([lo, hi], packed_dtype=_BF16)
        o_ref[pl.ds(i, TE_PACK, stride=TABLE_ROWS_PER_EXPERT), :] = pltpu.bitcast(words, jnp.uint32)


def _expert_table(t):
    return pl.pallas_call(
        _pack_table_kernel,
        grid=(N_EXPERTS // TE_PACK,),
        in_specs=[pl.BlockSpec((TE_PACK, D_MODEL), lambda i: (i, 0))],
        out_specs=pl.BlockSpec((TE_PACK * TABLE_ROWS_PER_EXPERT, LANES), lambda i: (i, 0)),
        out_shape=jax.ShapeDtypeStruct((N_EXPERTS * TABLE_ROWS_PER_EXPERT, LANES), jnp.uint32),
        compiler_params=pltpu.CompilerParams(
            dimension_semantics=("arbitrary",), vmem_limit_bytes=VMEM_LIMIT_DENSE),
        name="pack_table",
    )(t)


def kernel(x, norm1_g, w_in, sink_logits, pool_w, pool_scale, w_branch, w_gate, b_gate, w_out,
           norm2_g, peer_w_query, peer_sub_keys, peer_u, peer_v, norm_final_g):
    b, s, d = x.shape
    n = b * s
    assert norm1_g.shape[0] == 1 and d == D_MODEL
    assert s % TM_MIX == 0 and s % BLOCK == 0 and n % TM_PROJ == 0 and n % TM_ROUTE == 0 and n % TB_PEER == 0
    row = lambda a: a.reshape(1, -1)
    x2 = x.reshape(n, d)
    g1 = row(norm1_g[0])
    q, k, v, u = _in_proj(x2, g1, w_in[0].astype(_BF16))
    o_attn = _attention(q.reshape(b, s, ATTN_WIDTH), k.reshape(b, s, KV_WIDTH),
                        v.reshape(b, s, KV_WIDTH), sink_logits[0]).reshape(n, ATTN_WIDTH)
    x1, h2, h3 = _mix(x2, u, o_attn, g1, w_gate[0].astype(_BF16), row(b_gate[0]),
                  pool_w[0].astype(_BF16), row(pool_scale[0]), w_branch[0].astype(_BF16),
                  w_out[0].astype(_BF16), row(norm2_g[0]), s)
    idx, gate = _route(h2, peer_w_query[0].astype(_BF16), peer_sub_keys[0].astype(_BF16))
    act = _peer_up(idx, h3, gate, _expert_table(peer_u[0]))
    y = _peer_down(idx, act, x1, _expert_table(peer_v[0]), row(norm_final_g))
    return y.reshape(b, s, d)
```
